```python
import math
import jax
import jax.numpy as jnp
from jax import lax
import numpy as np

D_MODEL = 1024
BATCH = 8
SEQ = 8192
DEPTH = 1
DEC_BATCH = 128
DEC_SEQ = 1
PAST_LEN = 8192
PAGE_SIZE = 128

HEAD_DIM = 64
B_WIDTH = D_MODEL // 2
A_WIDTH = D_MODEL - B_WIDTH
H_B = B_WIDTH // HEAD_DIM
H_A = A_WIDTH // HEAD_DIM
MIX_WIDTH = A_WIDTH + B_WIDTH
IN_WIDTH = 3 * B_WIDTH + 2 * A_WIDTH
BLOCK = 256
TOPK_BLOCKS = 3
Q_BLOCK = 128
CHUNK = 128
ROPE_THETA = 10000.0
N_KEYS = 128
N_EXPERTS = N_KEYS * N_KEYS
P_HEADS = 8
KEY_DIM = 128
P_TOPK = 16
PEER_BLOCK = 128
PLE_DIM = 256
LN_EPS = 1e-5
ALPHA = (2.0 * DEPTH) ** 0.25
BETA = (8.0 * DEPTH) ** -0.25

kernel_name = 'hymba_moba_gmlp_peer_step'


def layer_norm(x, g, b):
    xf = x.astype(jnp.float32)
    mu = xf.mean(-1, keepdims=True)
    var = jnp.square(xf - mu).mean(-1, keepdims=True)
    return ((xf - mu) * lax.rsqrt(var + LN_EPS) * g + b).astype(x.dtype)


def rope(x, pos):
    half = x.shape[-1] // 2
    inv_freq = ROPE_THETA ** (-jnp.arange(half, dtype=jnp.float32) / half)
    ang = pos.astype(jnp.float32)[:, None] * inv_freq[None, :]
    cos = jnp.cos(ang)[None, :, None, :]
    sin = jnp.sin(ang)[None, :, None, :]
    xf = x.astype(jnp.float32)
    x1, x2 = xf[..., :half], xf[..., half:]
    return jnp.concatenate([x1 * cos - x2 * sin, x2 * cos + x1 * sin], -1).astype(x.dtype)


def mixer_inputs(x, w_in, vnorm_g, vnorm_b, pos):
    B, T, _ = x.shape
    z = x @ w_in
    q, k, v, u, cv = jnp.split(z, [B_WIDTH, 2 * B_WIDTH, 3 * B_WIDTH, 3 * B_WIDTH + A_WIDTH], axis=-1)
    q = rope(q.reshape(B, T, H_B, HEAD_DIM), pos)
    k = rope(k.reshape(B, T, H_B, HEAD_DIM), pos)
    v = v.reshape(B, T, H_B, HEAD_DIM)
    u = jax.nn.gelu(u, approximate=False).reshape(B, T, H_A, HEAD_DIM)
    cv = layer_norm(jax.nn.gelu(cv, approximate=False).reshape(B, T, H_A, HEAD_DIM), vnorm_g, vnorm_b)
    return q, k, v, u, cv


def chunk_mix(cv, w_s, b_s):
    B, T, H, hd = cv.shape
    vc = cv.reshape(B, T // CHUNK, CHUNK, H, hd)
    w = w_s * jnp.tril(jnp.ones((CHUNK, CHUNK), w_s.dtype))
    out = jnp.einsum('hts,bcshd->bcthd', w, vc) + b_s.T[None, None, :, :, None]
    return out.reshape(B, T, H, hd)


def moba_rows(q, kmean, q_pos):
    nb = kmean.shape[-3]
    s = jnp.einsum('...thd,...nhd->...thn', q.astype(jnp.float32), kmean.astype(jnp.float32))
    q_blk = q_pos // BLOCK
    fully_past = jnp.arange(nb)[None, :] < q_blk[:, None]
    s = jnp.where(fully_past[:, None, :], s, -jnp.inf)
    if nb < TOPK_BLOCKS:
        s = jnp.pad(s, [(0, 0)] * (s.ndim - 1) + [(0, TOPK_BLOCKS - nb)], constant_values=-jnp.inf)
    top_s, sel = lax.top_k(s, TOPK_BLOCKS)
    offs = jnp.arange(BLOCK)
    rows_sel = (sel[..., None] * BLOCK + offs).reshape(sel.shape[:-1] + (TOPK_BLOCKS * BLOCK,))
    ok_sel = jnp.repeat(jnp.isfinite(top_s), BLOCK, axis=-1)
    own = q_blk[:, None] * BLOCK + offs
    rows_own = jnp.broadcast_to(own[:, None, :], sel.shape[:-1] + (BLOCK,))
    ok_own = jnp.broadcast_to((own <= q_pos[:, None])[:, None, :], rows_own.shape)
    return (jnp.concatenate([rows_sel, rows_own], -1), jnp.concatenate([ok_sel, ok_own], -1))


def gathered_attention(q, kg, vg, ok):
    s = jnp.einsum('...hd,...hrd->...hr', q.astype(jnp.float32), kg.astype(jnp.float32)) / math.sqrt(q.shape[-1])
    p = jax.nn.softmax(jnp.where(ok, s, -jnp.inf), axis=-1)
    return jnp.einsum('...hr,...hrd->...hd', p, vg.astype(jnp.float32)).astype(q.dtype)


def prompt_moba(q, k, v):
    B, S, H, hd = q.shape
    nb = -(-S // BLOCK)
    kmean = jnp.pad(k.astype(jnp.float32), ((0, 0), (0, nb * BLOCK - S), (0, 0), (0, 0))
                    ).reshape(B, nb, BLOCK, H, hd).mean(axis=2)
    nqb = S // Q_BLOCK
    h_idx = jnp.arange(H)[None, :, None]

    def one_block(i):
        b = i // nqb
        start = (i % nqb) * Q_BLOCK
        q_pos = start + jnp.arange(Q_BLOCK)
        qi = lax.dynamic_slice_in_dim(q[b], start, Q_BLOCK, axis=0)
        rows, ok = moba_rows(qi, kmean[b], q_pos)
        rows = jnp.minimum(rows, S - 1)
        kb, vb = k[b], v[b]
        return gathered_attention(qi, kb[rows, h_idx], vb[rows, h_idx], ok)

    o = lax.map(one_block, jnp.arange(B * nqb))
    return o.reshape(B, S, H, hd)


def paged_block_means(cache_k, layer, page_table, k_new):
    DB, T, H, hd = k_new.shape
    past_sums = cache_k[layer, page_table].astype(jnp.float32).sum(axis=2)
    t_pad = -(-T // PAGE_SIZE) * PAGE_SIZE
    new_sums = jnp.pad(k_new.astype(jnp.float32), ((0, 0), (0, t_pad - T), (0, 0), (0, 0))
                       ).reshape(DB, t_pad // PAGE_SIZE, PAGE_SIZE, H, hd).sum(axis=2)
    sums = jnp.concatenate([past_sums, new_sums], axis=1)
    ppb = BLOCK // PAGE_SIZE
    ng = sums.shape[1]
    nb = -(-ng // ppb)
    sums = jnp.pad(sums, ((0, 0), (0, nb * ppb - ng), (0, 0), (0, 0))).reshape(DB, nb, ppb, H, hd).sum(axis=2)
    return sums / BLOCK


def gather_rows(pool, layer, page_table, new, rows):
    DB, T, H, _ = new.shape
    past = page_table.shape[1] * PAGE_SIZE
    b_idx = jnp.arange(DB)[:, None, None, None]
    h_idx = jnp.arange(H)[None, None, :, None]
    r_past = jnp.minimum(rows, past - 1)
    page = page_table[b_idx, r_past // PAGE_SIZE]
    from_past = pool[layer, page, r_past % PAGE_SIZE, h_idx]
    from_new = new[b_idx, jnp.clip(rows - past, 0, T - 1), h_idx]
    return jnp.where((rows < past)[..., None], from_past, from_new)


def sample_moba(q, k_new, v_new, cache_k, cache_v, layer, page_table):
    DB, T, H, hd = q.shape
    past = page_table.shape[1] * PAGE_SIZE
    total = past + T
    kmean = paged_block_means(cache_k, layer, page_table, k_new)
    q_pos = past + jnp.arange(T)
    rows, ok = moba_rows(q, kmean, q_pos)
    rows = jnp.minimum(rows, total - 1)
    kg = gather_rows(cache_k, layer, page_table, k_new, rows)
    vg = gather_rows(cache_v, layer, page_table, v_new, rows)
    return gathered_attention(q, kg, vg, ok)


def peer_ffn(x, w_q, sub_keys, u_tab, v_tab):
    n, d = x.shape
    n_pad = -(-n // PEER_BLOCK) * PEER_BLOCK
    xp = jnp.pad(x, ((0, n_pad - n), (0, 0))).reshape(n_pad // PEER_BLOCK, PEER_BLOCK, d)

    def step(xb):
        q = (xb @ w_q).reshape(PEER_BLOCK, P_HEADS, 2, KEY_DIM).astype(jnp.float32)
        s = jnp.einsum('nhcd,hckd->nhck', q, sub_keys.astype(jnp.float32))
        ts, ti = lax.top_k(s, P_TOPK)
        cand = (ts[:, :, 0, :, None] + ts[:, :, 1, None, :]).reshape(PEER_BLOCK, P_HEADS, P_TOPK * P_TOPK)
        cidx = (ti[:, :, 0, :, None] * N_KEYS + ti[:, :, 1, None, :]).reshape(PEER_BLOCK, P_HEADS, P_TOPK * P_TOPK)
        fs, fi = lax.top_k(cand, P_TOPK)
        eid = jnp.take_along_axis(cidx, fi, axis=-1)
        g = jax.nn.softmax(fs, axis=-1)
        a = jax.nn.gelu(jnp.einsum('nd,nhkd->nhk', xb, u_tab[eid]).astype(jnp.float32), approximate=False)
        return jnp.einsum('nhk,nhkd->nd', (g * a).astype(xb.dtype), v_tab[eid])

    out = lax.map(step, xp)
    return out.reshape(n_pad, d)[:n]


def finish_layer(x, attn, chunk, p, w_out, ln1_g, ln1_b, peer_wq, peer_subkeys, peer_u, peer_v,
                 ln2_g, ln2_b, ple_w, ple_gate_w, ple_gate_b):
    B, T, _ = x.shape
    heads = jnp.concatenate([attn.reshape(B, T, B_WIDTH), chunk.reshape(B, T, A_WIDTH)], axis=-1)
    x1 = layer_norm(ALPHA * x + heads @ w_out, ln1_g, ln1_b)
    ffn = peer_ffn(x1.reshape(B * T, D_MODEL), peer_wq, peer_subkeys, peer_u, peer_v).reshape(B, T, D_MODEL)
    x2 = layer_norm(ALPHA * x1 + ffn, ln2_g, ln2_b)
    gate = jax.nn.sigmoid(x2 @ ple_gate_w + ple_gate_b)
    return x2 + gate * (p @ ple_w)


def setup_inputs(seed: int = 0) -> dict:
    key = jax.random.key(seed)
    ks = jax.random.split(key, 26)
    n_pages = PAST_LEN // PAGE_SIZE
    n_used = DEC_BATCH * n_pages
    n_pool = n_used + max(1, n_used // 4)

    def nrm(k, shape, scale):
        return jax.random.normal(k, shape, jnp.float32) * scale

    page_table = jax.random.permutation(ks[4], n_pool)[:n_used].reshape(DEC_BATCH, n_pages).astype(jnp.int32)
    return {
        'x_prompt': nrm(ks[0], (BATCH, SEQ, D_MODEL), 1.0),
        'x_sample': nrm(ks[1], (DEC_BATCH, DEC_SEQ, D_MODEL), 1.0),
        'cache_k': nrm(ks[2], (DEPTH, n_pool, PAGE_SIZE, H_B, HEAD_DIM), 1.0),
        'cache_v': nrm(ks[3], (DEPTH, n_pool, PAGE_SIZE, H_B, HEAD_DIM), 1.0),
        'page_table': page_table,
        'p_prompt': nrm(ks[5], (DEPTH, BATCH, SEQ, PLE_DIM), 1.0),
        'p_sample': nrm(ks[6], (DEPTH, DEC_BATCH, DEC_SEQ, PLE_DIM), 1.0),
        'w_in': nrm(ks[7], (DEPTH, D_MODEL, IN_WIDTH), D_MODEL ** -0.5),
        'vnorm_g': 1.0 + nrm(ks[8], (DEPTH, H_A, HEAD_DIM), 0.02),
        'vnorm_b': nrm(ks[9], (DEPTH, H_A, HEAD_DIM), 0.02),
        'w_spatial': nrm(ks[10], (DEPTH, H_A, CHUNK, CHUNK), CHUNK ** -0.5),
        'b_spatial': 1.0 + nrm(ks[11], (DEPTH, H_A, CHUNK), 0.1),
        'w_out': nrm(ks[12], (DEPTH, MIX_WIDTH, D_MODEL), BETA * MIX_WIDTH ** -0.5),
        'ln1_g': 1.0 + nrm(ks[13], (DEPTH, D_MODEL), 0.02),
        'ln1_b': nrm(ks[14], (DEPTH, D_MODEL), 0.02),
        'peer_wq': nrm(ks[15], (DEPTH, D_MODEL, P_HEADS * 2 * KEY_DIM), D_MODEL ** -0.5),
        'peer_subkeys': nrm(ks[16], (DEPTH, P_HEADS, 2, N_KEYS, KEY_DIM), KEY_DIM ** -0.5),
        'peer_u': nrm(ks[17], (DEPTH, N_EXPERTS, D_MODEL), D_MODEL ** -0.5),
        'peer_v': nrm(ks[18], (DEPTH, N_EXPERTS, D_MODEL), BETA),
        'ln2_g': 1.0 + nrm(ks[19], (DEPTH, D_MODEL), 0.02),
        'ln2_b': nrm(ks[20], (DEPTH, D_MODEL), 0.02),
        'ple_w': nrm(ks[21], (DEPTH, PLE_DIM, D_MODEL), PLE_DIM ** -0.5),
        'ple_gate_w': nrm(ks[22], (DEPTH, D_MODEL, D_MODEL), D_MODEL ** -0.5),
        'ple_gate_b': nrm(ks[23], (DEPTH, D_MODEL), 0.02),
    }


def reference(x_prompt, x_sample, cache_k, cache_v, page_table, p_prompt, p_sample, w_in, vnorm_g, vnorm_b,
              w_spatial, b_spatial, w_out, ln1_g, ln1_b, peer_wq, peer_subkeys, peer_u, peer_v, ln2_g, ln2_b,
              ple_w, ple_gate_w, ple_gate_b):
    S = x_prompt.shape[1]
    T = x_sample.shape[1]
    past = page_table.shape[1] * PAGE_SIZE
    pos_p = jnp.arange(S)
    pos_s = past + jnp.arange(T)
    t_pad = -(-T // CHUNK) * CHUNK
    xp, xs = x_prompt, x_sample
    kp_l, vp_l, ks_l, vs_l, cs_l = [], [], [], [], []
    for i in range(DEPTH):
        qp, kp, vp, up, cvp = mixer_inputs(xp, w_in[i], vnorm_g[i], vnorm_b[i], pos_p)
        attn_p = prompt_moba(qp, kp, vp)
        chunk_p = up * chunk_mix(cvp, w_spatial[i], b_spatial[i])
        qs, kn, vn, us, cvs = mixer_inputs(xs, w_in[i], vnorm_g[i], vnorm_b[i], pos_s)
        attn_s = sample_moba(qs, kn, vn, cache_k, cache_v, i, page_table)
        cv_pad = jnp.pad(cvs, ((0, 0), (0, t_pad - T), (0, 0), (0, 0)))
        chunk_s = us * chunk_mix(cv_pad, w_spatial[i], b_spatial[i])[:, :T]
        xp = finish_layer(xp, attn_p, chunk_p, p_prompt[i], w_out[i], ln1_g[i], ln1_b[i], peer_wq[i],
                          peer_subkeys[i], peer_u[i], peer_v[i], ln2_g[i], ln2_b[i], ple_w[i],
                          ple_gate_w[i], ple_gate_b[i])
        xs = finish_layer(xs, attn_s, chunk_s, p_sample[i], w_out[i], ln1_g[i], ln1_b[i], peer_wq[i],
                          peer_subkeys[i], peer_u[i], peer_v[i], ln2_g[i], ln2_b[i], ple_w[i],
                          ple_gate_w[i], ple_gate_b[i])
        kp_l.append(kp)
        vp_l.append(vp)
        ks_l.append(kn)
        vs_l.append(vn)
        cs_l.append(cvs)
    k_prompt_new = jnp.stack(kp_l)
    v_prompt_new = jnp.stack(vp_l)
    k_sample_new = jnp.stack(ks_l)
    v_sample_new = jnp.stack(vs_l)
    chunk_v_sample_new = jnp.stack(cs_l)
    return (xp, xs, k_prompt_new, v_prompt_new, k_sample_new, v_sample_new, chunk_v_sample_new)
```

```python
import functools
import math

import jax
import jax.numpy as jnp
from jax import lax
from jax.experimental import pallas as pl
from jax.experimental.pallas import tpu as pltpu

HEAD_DIM = 64
BLOCK = 256
PAGE = 128
CHUNK = 128
TOPK_BLOCKS = 3
N_KEYS = 128
P_HEADS = 8
KEY_DIM = 128
P_TOPK = 16
LN_EPS = 1e-5
ROPE_THETA = 10000.0
LANES = 128
NEG = -1e30

VMEM_LIMIT = 56 * 1024 * 1024

F32 = jnp.float32
BF16 = jnp.bfloat16


def _nt(a, b):
    return lax.dot_general(a, b, (((1,), (1,)), ((), ())), preferred_element_type=F32)


def _nn(a, b):
    return jnp.dot(a, b, preferred_element_type=F32)


def _split(a):
    hi = a.astype(BF16)
    lo = (a - hi.astype(F32)).astype(BF16)
    return hi, lo


def _gelu(x):
    return 0.5 * x * (1.0 + lax.erf(x * (1.0 / math.sqrt(2.0))))


def _cparams(sem):
    return pltpu.CompilerParams(dimension_semantics=sem, vmem_limit_bytes=VMEM_LIMIT)


def _rope_cols(a, cos, sin_signed):
    outs = []
    lane = lax.broadcasted_iota(jnp.int32, (1, LANES), 1)
    first_half = (lane % HEAD_DIM) < (HEAD_DIM // 2)
    for c in range(a.shape[1] // LANES):
        ac = a[:, c * LANES:(c + 1) * LANES]
        partner = jnp.where(first_half, pltpu.roll(ac, LANES - HEAD_DIM // 2, 1), pltpu.roll(ac, HEAD_DIM // 2, 1))
        outs.append(ac * cos + partner * sin_signed)
    return jnp.concatenate(outs, axis=1)


def _mix_core(x_ref, w_ref, cos_ref, sin_ref, avg_ref, g_ref, b_ref, bw, aw):
    z = _nn(x_ref[...].astype(BF16), w_ref[...])
    cos = cos_ref[...]
    sin = sin_ref[...]
    q = _rope_cols(z[:, 0:bw], cos, sin)
    k = _rope_cols(z[:, bw:2 * bw], cos, sin)
    v = z[:, 2 * bw:3 * bw]
    u = _gelu(z[:, 3 * bw:3 * bw + aw])
    gv = _gelu(z[:, 3 * bw + aw:3 * bw + 2 * aw])
    avg = avg_ref[...]
    gh, gl = _split(gv)
    mu = _nn(gh, avg) + _nn(gl, avg)
    cen = gv - mu
    sh, sl = _split(cen * cen)
    var = _nn(sh, avg) + _nn(sl, avg)
    cvn = cen * lax.rsqrt(var + LN_EPS) * g_ref[...] + b_ref[...]
    return q, k, v, u, cvn


def _mix_prompt_kernel(x_ref, w_ref, cos_ref, sin_ref, avg_ref, g_ref, b_ref, wsp_ref, bsp_ref,
                       q_ref, k_ref, v_ref, vt_ref, ch_ref, *, bw, aw):
    q, k, v, u, cvn = _mix_core(x_ref, w_ref, cos_ref, sin_ref, avg_ref, g_ref, b_ref, bw, aw)
    q_ref[...] = q
    k_ref[...] = k
    v_ref[...] = v
    vt_ref[0] = v.T.astype(BF16)
    lane = lax.broadcasted_iota(jnp.int32, (1, LANES), 1)
    low = lane < HEAD_DIM
    cvb = cvn.astype(BF16)
    for c in range(cvn.shape[0] // CHUNK):
        rows = slice(c * CHUNK, (c + 1) * CHUNK)
        for p in range(aw // LANES):
            cols = slice(p * LANES, (p + 1) * LANES)
            cp = cvb[rows, cols]
            mixed = jnp.where(low, _nn(wsp_ref[2 * p], cp), _nn(wsp_ref[2 * p + 1], cp)) + bsp_ref[:, cols]
            ch_ref[rows, cols] = (u[rows, cols] * mixed).astype(BF16)


def _mix_sample_kernel(x_ref, w_ref, cos_ref, sin_ref, avg_ref, g_ref, b_ref, w00_ref, b0_ref,
                       q_ref, k_ref, v_ref, cv_ref, ch_ref, *, bw, aw):
    q, k, v, u, cvn = _mix_core(x_ref, w_ref, cos_ref, sin_ref, avg_ref, g_ref, b_ref, bw, aw)
    q_ref[...] = q
    k_ref[...] = k
    v_ref[...] = v
    cv_ref[...] = cvn
    ch_ref[...] = (u * (cvn * w00_ref[...] + b0_ref[...])).astype(BF16)


def _rope_tables(pos):
    half = HEAD_DIM // 2
    inv_freq = ROPE_THETA ** (-jnp.arange(half, dtype=F32) / half)
    ang = pos.astype(F32)[:, None] * inv_freq[None, :]
    cos, sin = jnp.cos(ang), jnp.sin(ang)
    cos_t = jnp.concatenate([cos, cos, cos, cos], axis=1)
    sin_t = jnp.concatenate([-sin, sin, -sin, sin], axis=1)
    return cos_t, sin_t


def _head_avg_matrix(width):
    h = jnp.arange(width) // HEAD_DIM
    return jnp.where(h[:, None] == h[None, :], 1.0 / HEAD_DIM, 0.0).astype(BF16)


def _mix_prompt(x2d, w_in_b, cos_t, sin_t, avg, g_row, b_row, wsp_b, bsp_full, batch, seq, bw, aw, tt):
    n, d = x2d.shape
    tiles_per_seq = seq // tt
    grid = (n // tt,)
    full = lambda shape: pl.BlockSpec(shape, lambda i: (0,) * len(shape))
    tok = lambda w: pl.BlockSpec((tt, w), lambda i: (i, 0))
    return pl.pallas_call(
        functools.partial(_mix_prompt_kernel, bw=bw, aw=aw),
        grid=grid,
        in_specs=[tok(d), full(w_in_b.shape),
                  pl.BlockSpec((tt, LANES), lambda i: (i % tiles_per_seq, 0)),
                  pl.BlockSpec((tt, LANES), lambda i: (i % tiles_per_seq, 0)),
                  full(avg.shape), full(g_row.shape), full(b_row.shape), full(wsp_b.shape), full(bsp_full.shape)],
        out_specs=[tok(bw), tok(bw), tok(bw),
                   pl.BlockSpec((1, bw, tt), lambda i: (i // tiles_per_seq, 0, i % tiles_per_seq)),
                   tok(aw)],
        out_shape=[jax.ShapeDtypeStruct((n, bw), F32), jax.ShapeDtypeStruct((n, bw), F32),
                   jax.ShapeDtypeStruct((n, bw), F32), jax.ShapeDtypeStruct((batch, bw, seq), BF16),
                   jax.ShapeDtypeStruct((n, aw), BF16)],
        compiler_params=_cparams(("arbitrary",)),
        name="mix_prompt",
    )(x2d, w_in_b, cos_t, sin_t, avg, g_row, b_row, wsp_b, bsp_full)


def _mix_sample(x2d, w_in_b, cos_t, sin_t, avg, g_row, b_row, w00_row, b0_row, bw, aw):
    n, d = x2d.shape
    full = lambda shape: pl.BlockSpec(shape, lambda i: (0,) * len(shape))
    ins = (x2d, w_in_b, cos_t, sin_t, avg, g_row, b_row, w00_row, b0_row)
    return pl.pallas_call(
        functools.partial(_mix_sample_kernel, bw=bw, aw=aw),
        grid=(1,),
        in_specs=[full(a.shape) for a in ins],
        out_specs=[full((n, bw))] * 3 + [full((n, aw))] * 2,
        out_shape=[jax.ShapeDtypeStruct((n, bw), F32)] * 3
                  + [jax.ShapeDtypeStruct((n, aw), F32), jax.ShapeDtypeStruct((n, aw), BF16)],
        compiler_params=_cparams(("arbitrary",)),
        name="mix_sample",
    )(*ins)


def _top_rows(sc, rows, k):
    sel = jnp.zeros(sc.shape, F32)
    big = float(sc.shape[0])
    for _ in range(k):
        m = jnp.max(sc, axis=0, keepdims=True)
        first = jnp.min(jnp.where(sc == m, rows, big), axis=0, keepdims=True)
        hit = rows == first
        sel = jnp.where(hit & (m > -jnp.inf), 1.0, sel)
        sc = jnp.where(hit, -jnp.inf, sc)
    return sel


def _moba_prompt_kernel(q_ref, k_ref, vt_ref, o_ref, kmean_scr, sel_scr, *, nblocks):
    i = pl.program_id(2)

    @pl.when(i == 0)
    def _():
        for nb in range(nblocks):
            kmean_scr[nb:nb + 1, :] = jnp.sum(k_ref[0, nb * BLOCK:(nb + 1) * BLOCK, :], axis=0, keepdims=True) * (1.0 / BLOCK)

    lane = lax.broadcasted_iota(jnp.int32, (1, LANES), 1)
    q2 = q_ref[...]
    kmh, kml = _split(kmean_scr[...])
    blk = lax.broadcasted_iota(jnp.int32, (nblocks, BLOCK), 0).astype(F32)
    qs = []
    for h in range(2):
        qm = jnp.where((lane // HEAD_DIM) == h, q2, 0.0)
        qh, ql = _split(qm)
        sc = _nt(kmh, qh) + _nt(kmh, ql) + _nt(kml, qh)
        sc = jnp.where(blk < i.astype(F32), sc, -jnp.inf)
        sel_scr[h] = _top_rows(sc, blk, TOPK_BLOCKS)
        qs.append((qm * (1.0 / math.sqrt(HEAD_DIM))).astype(BF16))

    def attend(s, vtb, carry):
        m, l, acc = carry
        m_new = jnp.maximum(m, jnp.max(s, axis=0, keepdims=True))
        alpha = jnp.exp(m - m_new)
        p = jnp.exp(s - m_new)
        l = alpha * l + jnp.sum(p, axis=0, keepdims=True)
        acc = alpha * acc + _nn(vtb, p.astype(BF16))
        return m_new, l, acc

    def body(j, carry):
        start = pl.multiple_of(j * BLOCK, BLOCK)
        kb = k_ref[0, pl.ds(start, BLOCK), :].astype(BF16)
        vtb = vt_ref[0, :, pl.ds(start, BLOCK)]
        out = []
        for h in range(2):
            s = _nt(kb, qs[h])
            s = jnp.where(sel_scr[h, pl.ds(j, 1), :] > 0.0, s, NEG)
            out.append(attend(s, vtb[h * HEAD_DIM:(h + 1) * HEAD_DIM, :], carry[h]))
        return tuple(out)

    init = tuple((jnp.full((1, BLOCK), NEG, F32), jnp.zeros((1, BLOCK), F32), jnp.zeros((HEAD_DIM, BLOCK), F32))
                 for _ in range(2))
    carry = lax.fori_loop(0, i, body, init)

    start = pl.multiple_of(i * BLOCK, BLOCK)
    kb = k_ref[0, pl.ds(start, BLOCK), :].astype(BF16)
    vtb = vt_ref[0, :, pl.ds(start, BLOCK)]
    key_row = lax.broadcasted_iota(jnp.int32, (BLOCK, BLOCK), 0)
    qry_col = lax.broadcasted_iota(jnp.int32, (BLOCK, BLOCK), 1)
    outs = []
    for h in range(2):
        s = jnp.where(key_row <= qry_col, _nt(kb, qs[h]), NEG)
        m, l, acc = attend(s, vtb[h * HEAD_DIM:(h + 1) * HEAD_DIM, :], carry[h])
        outs.append(acc / l)
    o_ref[...] = jnp.concatenate(outs, axis=0).T.astype(BF16)


def _moba_prompt(q, k, vt, batch, seq, bw):
    n = q.shape[0]
    nblocks = seq // BLOCK
    pairs = bw // LANES
    k3 = k.reshape(batch, seq, bw)
    return pl.pallas_call(
        functools.partial(_moba_prompt_kernel, nblocks=nblocks),
        grid=(batch, pairs, nblocks),
        in_specs=[pl.BlockSpec((BLOCK, LANES), lambda b, p, i: (b * nblocks + i, p)),
                  pl.BlockSpec((1, seq, LANES), lambda b, p, i: (b, 0, p)),
                  pl.BlockSpec((1, LANES, seq), lambda b, p, i: (b, p, 0))],
        out_specs=pl.BlockSpec((BLOCK, LANES), lambda b, p, i: (b * nblocks + i, p)),
        out_shape=jax.ShapeDtypeStruct((n, bw), BF16),
        scratch_shapes=[pltpu.VMEM((nblocks, LANES), F32), pltpu.VMEM((2, nblocks, BLOCK), F32)],
        compiler_params=_cparams(("arbitrary", "arbitrary", "arbitrary")),
        name="moba_prompt",
    )(q, k3, vt)


PAGES_PER_STEP = 16


def _pages_kernel(pt_ref, q_ref, hsel_ref, *rest, nblocks, heads):
    page_refs = rest[:PAGES_PER_STEP]
    sel_ref = rest[PAGES_PER_STEP]
    sums_scr = rest[PAGES_PER_STEP + 1]
    g = pl.program_id(1)
    ppb = BLOCK // PAGE
    bps = PAGES_PER_STEP // ppb
    parts = []
    for b in range(bps):
        acc = jnp.sum(page_refs[b * ppb][0, 0], axis=0, keepdims=True)
        for r in range(1, ppb):
            acc = acc + jnp.sum(page_refs[b * ppb + r][0, 0], axis=0, keepdims=True)
        parts.append(acc)
    sums_scr[pl.ds(pl.multiple_of(g * bps, bps), bps), :] = jnp.concatenate(parts, axis=0)

    @pl.when(g == pl.num_programs(1) - 1)
    def _():
        kmean = sums_scr[...] * (1.0 / BLOCK)
        prod = kmean * q_ref[0]
        ph, pl_ = _split(prod)
        hs = hsel_ref[...]
        sc = _nn(ph, hs) + _nn(pl_, hs)
        rows = lax.broadcasted_iota(jnp.int32, sc.shape, 0).astype(F32)
        big = float(nblocks)
        picks = []
        for _ in range(TOPK_BLOCKS):
            m = jnp.max(sc, axis=0, keepdims=True)
            first = jnp.min(jnp.where(sc == m, rows, big), axis=0, keepdims=True)
            picks.append(first)
            sc = jnp.where(rows == first, -jnp.inf, sc)
        picks.append(jnp.zeros((8 - TOPK_BLOCKS, LANES), F32))
        sel_ref[0] = jnp.concatenate(picks, axis=0).astype(jnp.int32)


def _sample_block_choice(cache_k3, layer, pt_flat, q_s, hsel, n_seq, n_pages, width):
    nblocks = n_pages * PAGE // BLOCK
    steps = n_pages // PAGES_PER_STEP

    def page_spec(r):
        return pl.BlockSpec((1, 1, PAGE, width),
                            lambda s, g, pt: (layer, pt[s * n_pages + g * PAGES_PER_STEP + r], 0, 0))

    grid_spec = pltpu.PrefetchScalarGridSpec(
        num_scalar_prefetch=1,
        grid=(n_seq, steps),
        in_specs=[pl.BlockSpec((1, 1, width), lambda s, g, pt: (s, 0, 0)),
                  pl.BlockSpec(hsel.shape, lambda s, g, pt: (0, 0))]
                 + [page_spec(r) for r in range(PAGES_PER_STEP)],
        out_specs=pl.BlockSpec((1, 8, LANES), lambda s, g, pt: (s, 0, 0)),
        scratch_shapes=[pltpu.VMEM((nblocks, width), F32)],
    )
    return pl.pallas_call(
        functools.partial(_pages_kernel, nblocks=nblocks, heads=width // HEAD_DIM),
        grid_spec=grid_spec,
        out_shape=jax.ShapeDtypeStruct((n_seq, 8, LANES), jnp.int32),
        compiler_params=_cparams(("arbitrary", "arbitrary")),
        name="sample_block_choice",
    )(pt_flat, q_s.reshape(n_seq, 1, width), hsel, *([cache_k3] * PAGES_PER_STEP))


def _sattn_kernel(pt_ref, sel_ref, q_ref, kn_ref, vn_ref, *rest, n_fetch):
    k_refs = rest[:n_fetch]
    v_refs = rest[n_fetch:2 * n_fetch]
    o_ref = rest[2 * n_fetch]
    h = pl.program_id(1)
    width = q_ref.shape[-1]
    lane = lax.broadcasted_iota(jnp.int32, (1, width), 1)
    mine = (lane // HEAD_DIM) == h
    qm = jnp.where(mine, q_ref[0], 0.0) * (1.0 / math.sqrt(HEAD_DIM))
    q8 = jnp.broadcast_to(qm, (8, width)).astype(BF16)
    s_pages = [_nt(q8, kr[0, 0].astype(BF16))[0:1, :] for kr in k_refs]
    s_own = jnp.sum(qm.astype(BF16).astype(F32) * kn_ref[0].astype(BF16).astype(F32), axis=1, keepdims=True)
    m = s_own
    for s in s_pages:
        m = jnp.maximum(m, jnp.max(s, axis=1, keepdims=True))
    p_own = jnp.exp(s_own - m)
    l = p_own
    acc = p_own * vn_ref[0].astype(BF16).astype(F32)
    for s, vr in zip(s_pages, v_refs):
        p = jnp.exp(s - m)
        l = l + jnp.sum(p, axis=1, keepdims=True)
        p8 = jnp.broadcast_to(p, (8, PAGE)).astype(BF16)
        acc = acc + _nn(p8, vr[0, 0].astype(BF16))[0:1, :]
    res = (acc / l).astype(BF16)

    @pl.when(h == 0)
    def _():
        o_ref[0] = res

    @pl.when(h != 0)
    def _():
        o_ref[0] = jnp.where(mine, res, o_ref[0])


def _sample_attention(cache_k3, cache_v3, layer, pt_flat, sel_flat, q_s, k_s, v_s, n_seq, n_pages, width):
    heads = width // HEAD_DIM
    ppb = BLOCK // PAGE
    n_fetch = TOPK_BLOCKS * ppb

    def page_spec(r):
        pick, pg = divmod(r, ppb)
        return pl.BlockSpec(
            (1, 1, PAGE, width),
            lambda s, h, pt, sel: (layer, pt[s * n_pages + sel[(s * TOPK_BLOCKS + pick) * heads + h] * ppb + pg], 0, 0))

    row = pl.BlockSpec((1, 1, width), lambda s, h, pt, sel: (s, 0, 0))
    grid_spec = pltpu.PrefetchScalarGridSpec(
        num_scalar_prefetch=2,
        grid=(n_seq, heads),
        in_specs=[row, row, row] + [page_spec(r) for r in range(n_fetch)] * 2,
        out_specs=row,
    )
    r3 = lambda a: a.reshape(n_seq, 1, width)
    return pl.pallas_call(
        functools.partial(_sattn_kernel, n_fetch=n_fetch),
        grid_spec=grid_spec,
        out_shape=jax.ShapeDtypeStruct((n_seq, 1, width), BF16),
        compiler_params=_cparams(("arbitrary", "arbitrary")),
        name="sample_attention",
    )(pt_flat, sel_flat, r3(q_s), r3(k_s), r3(v_s), *([cache_k3] * n_fetch), *([cache_v3] * n_fetch)).reshape(n_seq, width)


def _cand_layout():
    a_of, b_of, valid = [], [], []
    def add(a, b):
        a_of.append(a)
        b_of.append(b)
        valid.append((a + 1) * (b + 1) <= P_TOPK)
    for b in range(16):
        add(0, b)
    for a in range(1, 8):
        for b in range(8):
            add(a, b)
    for a in range(8, 16):
        add(a, 0)
    return a_of, b_of, valid


_CAND_A, _CAND_B, _CAND_VALID = _cand_layout()
N_CAND = len(_CAND_A)


def _extract_top16(s, idx, ts_scr, slot, extra=None):
    big = float(1 << 20)
    t = s.shape[1]
    a_iota = lax.broadcasted_iota(jnp.int32, (P_TOPK, t), 0).astype(F32)

    def body(r, carry):
        s, rank, cut, mass, m0 = carry
        m = jnp.max(s, axis=0, keepdims=True)
        first = jnp.min(jnp.where(s == m, idx, big), axis=0, keepdims=True)
        hit = idx == first
        rank = jnp.where(hit, r.astype(F32), rank)
        s = jnp.where(hit, -jnp.inf, s)
        if extra is None:
            ts_scr[slot, pl.ds(r, 1), :] = m
        else:
            a_pick = jnp.max(jnp.where(hit, extra, -1.0), axis=0, keepdims=True)
            cut = cut + jnp.where(a_iota == a_pick, 1.0, 0.0)
            m0 = jnp.where(r == 0, m, m0)
            mass = mass + jnp.exp(m - m0)
        return s, rank, cut, mass, m0

    init = (s, jnp.full(s.shape, float(P_TOPK), F32), jnp.zeros((P_TOPK, t), F32),
            jnp.zeros((1, t), F32), jnp.zeros((1, t), F32))
    _, rank, cut, mass, _ = lax.fori_loop(0, P_TOPK, body, init)
    return rank, cut, mass


def _post_kernel(attn_ref, ch_ref, x_ref, woa_ref, wob_ref, g1_ref, b1_ref, wqh_ref, wql_ref, skh_ref, skl_ref,
                 ca_ref, cidx_ref, cval_ref,
                 x1t_ref, r1_ref, p1_ref, cut_ref, p0_ref, s_scr, rank_scr, ts_scr, *, alpha):
    z = _nn(attn_ref[...], woa_ref[...]) + _nn(ch_ref[...], wob_ref[...]) + alpha * x_ref[...]
    mu = jnp.mean(z, axis=1, keepdims=True)
    cen = z - mu
    var = jnp.mean(cen * cen, axis=1, keepdims=True)
    x1 = cen * lax.rsqrt(var + LN_EPS) * g1_ref[...] + b1_ref[...]
    x1t_ref[...] = x1.T
    xh, xl = _split(x1)
    t = x1.shape[0]
    key_row = lax.broadcasted_iota(jnp.int32, (N_KEYS, t), 0).astype(F32)

    def per_head(h, _):
        for c in range(2):
            rows = pl.ds(pl.multiple_of((h * 2 + c) * KEY_DIM, KEY_DIM), KEY_DIM)
            wh = wqh_ref[rows, :]
            wl = wql_ref[rows, :]
            qt = _nt(wh, xh) + _nt(wh, xl) + _nt(wl, xh)
            qh, ql = _split(qt)
            sh = skh_ref[h * 2 + c]
            sl = skl_ref[h * 2 + c]
            st = _nn(sh, qh) + _nn(sh, ql) + _nn(sl, qh)
            s_scr[c] = st
            rank, _, _ = _extract_top16(st, key_row, ts_scr, c)
            rank_scr[c] = rank
        ts0 = ts_scr[0]
        ts1 = ts_scr[1]
        pieces = [ts0[0:1, :] + ts1]
        for a in range(1, 8):
            pieces.append(ts0[a:a + 1, :] + ts1[0:8, :])
        pieces.append(ts0[8:16, :] + ts1[0:1, :])
        cand = jnp.where(cval_ref[...] > 0, jnp.concatenate(pieces, axis=0), -jnp.inf)
        cidx = jnp.broadcast_to(cidx_ref[...], cand.shape)
        ca = jnp.broadcast_to(ca_ref[...], cand.shape)
        _, cut, mass = _extract_top16(cand, cidx, ts_scr, 0, extra=ca)
        rank0 = rank_scr[0]
        rank1 = rank_scr[1]
        cutd = jnp.zeros((N_KEYS, t), F32)
        for a in range(P_TOPK):
            cutd = jnp.where(rank0 == float(a), cut[a:a + 1, :], cutd)
        p0 = jnp.where(rank0 < float(P_TOPK), jnp.exp(s_scr[0] - ts0[0:1, :]), 0.0)
        p1 = jnp.where(rank1 < float(P_TOPK), jnp.exp(s_scr[1] - ts1[0:1, :]), 0.0) / mass
        r1_ref[h] = rank1.astype(BF16)
        p1_ref[h] = p1.astype(BF16)
        cut_ref[h] = cutd
        p0_ref[h] = p0
        return 0

    lax.fori_loop(0, P_HEADS, per_head, 0)


def _post(attn, ch, x2d, woa, wob, g1, b1, wqh, wql, skh, skl, alpha, tb):
    n, d = x2d.shape
    bw, aw = attn.shape[1], ch.shape[1]
    ca = jnp.asarray(_CAND_A, F32).reshape(N_CAND, 1)
    cidx = jnp.asarray([a * P_TOPK + b for a, b in zip(_CAND_A, _CAND_B)], F32).reshape(N_CAND, 1)
    cval = jnp.asarray(_CAND_VALID, F32).reshape(N_CAND, 1)
    full = lambda a: pl.BlockSpec(a.shape, lambda i: (0,) * a.ndim)
    tok = lambda w: pl.BlockSpec((tb, w), lambda i: (i, 0))
    gate_spec = pl.BlockSpec((P_HEADS, N_KEYS, tb), lambda i: (0, 0, i))
    return pl.pallas_call(
        functools.partial(_post_kernel, alpha=alpha),
        grid=(n // tb,),
        in_specs=[tok(bw), tok(aw), tok(d), full(woa), full(wob), full(g1), full(b1), full(wqh), full(wql),
                  full(skh), full(skl), full(ca), full(cidx), full(cval)],
        out_specs=[pl.BlockSpec((d, tb), lambda i: (0, i)), gate_spec, gate_spec, gate_spec, gate_spec],
        out_shape=[jax.ShapeDtypeStruct((d, n), F32),
                   jax.ShapeDtypeStruct((P_HEADS, N_KEYS, n), BF16), jax.ShapeDtypeStruct((P_HEADS, N_KEYS, n), BF16),
                   jax.ShapeDtypeStruct((P_HEADS, N_KEYS, n), F32), jax.ShapeDtypeStruct((P_HEADS, N_KEYS, n), F32)],
        scratch_shapes=[pltpu.VMEM((2, N_KEYS, tb), F32), pltpu.VMEM((2, N_KEYS, tb), F32),
                        pltpu.VMEM((2, P_TOPK, tb), F32)],
        compiler_params=_cparams(("arbitrary",)),
        name="post",
    )(attn, ch, x2d, woa, wob, g1, b1, wqh, wql, skh, skl, ca, cidx, cval)


EXPERT_TILE = 1024


def _peer_kernel(x1t_ref, r1_ref, p1_ref, cut_ref, p0_ref, u_ref, vt_ref, p_ref, g2_ref, b2_ref,
                 gwt_ref, gb_ref, pwt_ref, y_ref, acc_scr, xb_scr, g_scr, *, alpha):
    e = pl.program_id(1)

    @pl.when(e == 0)
    def _():
        acc_scr[...] = jnp.zeros(acc_scr.shape, F32)
        xb_scr[...] = x1t_ref[...].astype(BF16)

    ht = _nn(u_ref[...], xb_scr[...])
    for ib in range(EXPERT_TILE // N_KEYS):
        w = None
        for h in range(P_HEADS):
            cut_row = cut_ref[h, ib:ib + 1, :].astype(BF16)
            p0_row = p0_ref[h, ib:ib + 1, :].astype(BF16)
            term = jnp.where(r1_ref[h] < cut_row, p1_ref[h], jnp.zeros((), BF16)) * p0_row
            w = term if w is None else w + term
        rows = slice(ib * N_KEYS, (ib + 1) * N_KEYS)
        g_scr[rows, :] = (_gelu(ht[rows, :]) * w.astype(F32)).astype(BF16)
    acc_scr[...] += _nn(vt_ref[...], g_scr[...])

    @pl.when(e == pl.num_programs(1) - 1)
    def _():
        z = alpha * x1t_ref[...] + acc_scr[...]
        mu = jnp.mean(z, axis=0, keepdims=True)
        cen = z - mu
        var = jnp.mean(cen * cen, axis=0, keepdims=True)
        x2 = cen * lax.rsqrt(var + LN_EPS) * g2_ref[...] + b2_ref[...]
        gate = jax.nn.sigmoid(_nn(gwt_ref[...], x2.astype(BF16)) + gb_ref[...])
        ple = _nt(pwt_ref[...], p_ref[...].astype(BF16))
        y_ref[...] = (x2 + gate * ple).T


def _peer(x1t, r1, p1, cut, p0, u_b, vt_b, p2d, g2c, b2c, gwt, gbc, pwt, alpha, tb):
    d, n = x1t.shape
    n_exp = u_b.shape[0]
    kpt = EXPERT_TILE // N_KEYS
    tcol = lambda rows: pl.BlockSpec((rows, tb), lambda t, e: (0, t))
    full = lambda a: pl.BlockSpec(a.shape, lambda t, e: (0,) * a.ndim)
    return pl.pallas_call(
        functools.partial(_peer_kernel, alpha=alpha),
        grid=(n // tb, n_exp // EXPERT_TILE),
        in_specs=[tcol(d),
                  pl.BlockSpec((P_HEADS, N_KEYS, tb), lambda t, e: (0, 0, t)),
                  pl.BlockSpec((P_HEADS, N_KEYS, tb), lambda t, e: (0, 0, t)),
                  pl.BlockSpec((P_HEADS, kpt, tb), lambda t, e: (0, e, t)),
                  pl.BlockSpec((P_HEADS, kpt, tb), lambda t, e: (0, e, t)),
                  pl.BlockSpec((EXPERT_TILE, d), lambda t, e: (e, 0)),
                  pl.BlockSpec((d, EXPERT_TILE), lambda t, e: (0, e)),
                  pl.BlockSpec((tb, p2d.shape[1]), lambda t, e: (t, 0)),
                  full(g2c), full(b2c), full(gwt), full(gbc), full(pwt)],
        out_specs=pl.BlockSpec((tb, d), lambda t, e: (t, 0)),
        out_shape=jax.ShapeDtypeStruct((n, d), F32),
        scratch_shapes=[pltpu.VMEM((d, tb), F32), pltpu.VMEM((d, tb), BF16), pltpu.VMEM((EXPERT_TILE, tb), BF16)],
        compiler_params=_cparams(("arbitrary", "arbitrary")),
        name="peer",
    )(x1t, r1, p1, cut, p0, u_b, vt_b, p2d, g2c, b2c, gwt, gbc, pwt)


def _finish(x2d, attn, ch, p2d, prm, alpha, tb):
    x1t, r1, p1, cut, p0 = _post(attn, ch, x2d, prm["woa"], prm["wob"], prm["g1"], prm["b1"], prm["wqh"], prm["wql"],
                                 prm["skh"], prm["skl"], alpha, tb)
    return _peer(x1t, r1, p1, cut, p0, prm["u_b"], prm["vt_b"], p2d, prm["g2c"], prm["b2c"], prm["gwt"],
                 prm["gbc"], prm["pwt"], alpha, tb)


def kernel(x_prompt, x_sample, cache_k, cache_v, page_table, p_prompt, p_sample, w_in, vnorm_g, vnorm_b, w_spatial, b_spatial, w_out, ln1_g, ln1_b, peer_wq, peer_subkeys, peer_u, peer_v, ln2_g, ln2_b, ple_w, ple_gate_w, ple_gate_b):
    batch, seq, d = x_prompt.shape
    n_seq, t_new, _ = x_sample.shape
    depth = w_in.shape[0]
    n_pages = page_table.shape[1]
    past = n_pages * PAGE
    h_a = vnorm_g.shape[1]
    aw = h_a * HEAD_DIM
    bw = (w_in.shape[2] - 2 * aw) // 3
    h_b = bw // HEAD_DIM
    assert t_new == 1 and seq % BLOCK == 0 and past % BLOCK == 0 and n_pages % PAGES_PER_STEP == 0
    assert bw % LANES == 0 and aw % LANES == 0 and h_b <= LANES
    alpha = (2.0 * depth) ** 0.25
    n_pool = cache_k.shape[1]

    tt = 512 if seq % 512 == 0 else BLOCK
    tb_p = 512 if (batch * seq) % 512 == 0 else BLOCK

    cos_p, sin_p = _rope_tables(jnp.arange(seq))
    cos_s, sin_s = _rope_tables(jnp.full((n_seq,), past, jnp.int32))
    avg = _head_avg_matrix(aw)
    hsel = (jnp.arange(bw)[:, None] // HEAD_DIM == jnp.arange(LANES)[None, :]).astype(BF16)
    tril = jnp.tril(jnp.ones((CHUNK, CHUNK), F32))
    cache_k3 = cache_k.reshape(depth, n_pool, PAGE, bw)
    cache_v3 = cache_v.reshape(depth, n_pool, PAGE, bw)
    pt_flat = page_table.reshape(-1).astype(jnp.int32)

    xp = x_prompt.reshape(batch * seq, d)
    xs = x_sample.reshape(n_seq, d)
    kp_l, vp_l, ks_l, vs_l, cs_l = [], [], [], [], []
    for i in range(depth):
        w_in_b = w_in[i].astype(BF16)
        g_row = vnorm_g[i].reshape(1, aw)
        b_row = vnorm_b[i].reshape(1, aw)
        wsp_b = (w_spatial[i] * tril).astype(BF16)
        bsp_full = jnp.repeat(b_spatial[i].T, HEAD_DIM, axis=1)
        w00_row = jnp.repeat(w_spatial[i][:, 0, 0], HEAD_DIM).reshape(1, aw)
        b0_row = jnp.repeat(b_spatial[i][:, 0], HEAD_DIM).reshape(1, aw)
        wq_t = peer_wq[i].T
        wqh, wql = _split(wq_t)
        skh, skl = _split(peer_subkeys[i].reshape(P_HEADS * 2, N_KEYS, KEY_DIM))
        prm = dict(
            woa=w_out[i][:bw].astype(BF16), wob=w_out[i][bw:].astype(BF16),
            g1=ln1_g[i].reshape(1, d), b1=ln1_b[i].reshape(1, d), wqh=wqh, wql=wql, skh=skh, skl=skl,
            u_b=peer_u[i].astype(BF16), vt_b=peer_v[i].T.astype(BF16),
            g2c=ln2_g[i].reshape(d, 1), b2c=ln2_b[i].reshape(d, 1),
            gwt=ple_gate_w[i].T.astype(BF16), gbc=ple_gate_b[i].reshape(d, 1), pwt=ple_w[i].T.astype(BF16))

        qp, kp, vp, vtp, chp = _mix_prompt(xp, w_in_b, cos_p, sin_p, avg, g_row, b_row, wsp_b, bsp_full,
                                           batch, seq, bw, aw, tt)
        attn_p = _moba_prompt(qp, kp, vtp, batch, seq, bw)
        qs, kn, vn, cvs, chs = _mix_sample(xs, w_in_b, cos_s, sin_s, avg, g_row, b_row, w00_row, b0_row, bw, aw)
        sel = _sample_block_choice(cache_k3, i, pt_flat, qs, hsel, n_seq, n_pages, bw)
        sel_flat = sel[:, :TOPK_BLOCKS, :h_b].reshape(-1)
        attn_s = _sample_attention(cache_k3, cache_v3, i, pt_flat, sel_flat, qs, kn, vn, n_seq, n_pages, bw)

        xp = _finish(xp, attn_p, chp, p_prompt[i].reshape(batch * seq, -1), prm, alpha, tb_p)
        xs = _finish(xs, attn_s, chs, p_sample[i].reshape(n_seq, -1), prm, alpha, n_seq)
        kp_l.append(kp.reshape(batch, seq, h_b, HEAD_DIM))
        vp_l.append(vp.reshape(batch, seq, h_b, HEAD_DIM))
        ks_l.append(kn.reshape(n_seq, 1, h_b, HEAD_DIM))
        vs_l.append(vn.reshape(n_seq, 1, h_b, HEAD_DIM))
        cs_l.append(cvs.reshape(n_seq, 1, h_a, HEAD_DIM))
    return (xp.reshape(batch, seq, d), xs.reshape(n_seq, 1, d), jnp.stack(kp_l), jnp.stack(vp_l),
            jnp.stack(ks_l), jnp.stack(vs_l), jnp.stack(cs_l))
```

```python
import functools
import math

import jax
import jax.numpy as jnp
from jax import lax
from jax.experimental import pallas as pl
from jax.experimental.pallas import tpu as pltpu

HEAD_DIM = 64
BLOCK = 256
PAGE = 128
CHUNK = 128
TOPK_BLOCKS = 3
N_KEYS = 128
P_HEADS = 8
KEY_DIM = 128
P_TOPK = 16
LN_EPS = 1e-5
ROPE_THETA = 10000.0
LANES = 128
NEG = -1e30

VMEM_LIMIT = 56 * 1024 * 1024

F32 = jnp.float32
BF16 = jnp.bfloat16


def _nt(a, b):
    return lax.dot_general(a, b, (((1,), (1,)), ((), ())), preferred_element_type=F32)


def _nn(a, b):
    return jnp.dot(a, b, preferred_element_type=F32)


def _split(a):
    hi = a.astype(BF16)
    lo = (a - hi.astype(F32)).astype(BF16)
    return hi, lo


def _gelu(x):
    return 0.5 * x * (1.0 + lax.erf(x * (1.0 / math.sqrt(2.0))))


def _cparams(sem):
    return pltpu.CompilerParams(dimension_semantics=sem, vmem_limit_bytes=VMEM_LIMIT)


def _rope_cols(a, cos, sin_signed):
    outs = []
    lane = lax.broadcasted_iota(jnp.int32, (1, LANES), 1)
    first_half = (lane % HEAD_DIM) < (HEAD_DIM // 2)
    for c in range(a.shape[1] // LANES):
        ac = a[:, c * LANES:(c + 1) * LANES]
        partner = jnp.where(first_half, pltpu.roll(ac, LANES - HEAD_DIM // 2, 1), pltpu.roll(ac, HEAD_DIM // 2, 1))
        outs.append(ac * cos + partner * sin_signed)
    return jnp.concatenate(outs, axis=1)


def _mix_core(x_ref, w_ref, cos_ref, sin_ref, avg_ref, g_ref, b_ref, bw, aw):
    z = _nn(x_ref[...].astype(BF16), w_ref[...])
    cos = cos_ref[...]
    sin = sin_ref[...]
    q = _rope_cols(z[:, 0:bw], cos, sin)
    k = _rope_cols(z[:, bw:2 * bw], cos, sin)
    v = z[:, 2 * bw:3 * bw]
    u = _gelu(z[:, 3 * bw:3 * bw + aw])
    gv = _gelu(z[:, 3 * bw + aw:3 * bw + 2 * aw])
    avg = avg_ref[...]
    gh, gl = _split(gv)
    mu = _nn(gh, avg) + _nn(gl, avg)
    cen = gv - mu
    sh, sl = _split(cen * cen)
    var = _nn(sh, avg) + _nn(sl, avg)
    cvn = cen * lax.rsqrt(var + LN_EPS) * g_ref[...] + b_ref[...]
    return q, k, v, u, cvn


def _mix_prompt_kernel(x_ref, w_ref, cos_ref, sin_ref, avg_ref, g_ref, b_ref, wsp_ref, bsp_ref,
                       q_ref, k_ref, v_ref, vt_ref, ch_ref, *, bw, aw):
    q, k, v, u, cvn = _mix_core(x_ref, w_ref, cos_ref, sin_ref, avg_ref, g_ref, b_ref, bw, aw)
    q_ref[...] = q
    k_ref[...] = k
    v_ref[...] = v
    vt_ref[0] = v.T.astype(BF16)
    lane = lax.broadcasted_iota(jnp.int32, (1, LANES), 1)
    low = lane < HEAD_DIM
    cvb = cvn.astype(BF16)
    for c in range(cvn.shape[0] // CHUNK):
        rows = slice(c * CHUNK, (c + 1) * CHUNK)
        for p in range(aw // LANES):
            cols = slice(p * LANES, (p + 1) * LANES)
            cp = cvb[rows, cols]
            mixed = jnp.where(low, _nn(wsp_ref[2 * p], cp), _nn(wsp_ref[2 * p + 1], cp)) + bsp_ref[:, cols]
            ch_ref[rows, cols] = (u[rows, cols] * mixed).astype(BF16)


def _mix_sample_kernel(x_ref, w_ref, cos_ref, sin_ref, avg_ref, g_ref, b_ref, w00_ref, b0_ref,
                       q_ref, k_ref, v_ref, cv_ref, ch_ref, *, bw, aw):
    q, k, v, u, cvn = _mix_core(x_ref, w_ref, cos_ref, sin_ref, avg_ref, g_ref, b_ref, bw, aw)
    q_ref[...] = q
    k_ref[...] = k
    v_ref[...] = v
    cv_ref[...] = cvn
    ch_ref[...] = (u * (cvn * w00_ref[...] + b0_ref[...])).astype(BF16)


def _rope_tables(pos):
    half = HEAD_DIM // 2
    inv_freq = ROPE_THETA ** (-jnp.arange(half, dtype=F32) / half)
    ang = pos.astype(F32)[:, None] * inv_freq[None, :]
    cos, sin = jnp.cos(ang), jnp.sin(ang)
    cos_t = jnp.concatenate([cos, cos, cos, cos], axis=1)
    sin_t = jnp.concatenate([-sin, sin, -sin, sin], axis=1)
    return cos_t, sin_t


def _head_avg_matrix(width):
    h = jnp.arange(width) // HEAD_DIM
    return jnp.where(h[:, None] == h[None, :], 1.0 / HEAD_DIM, 0.0).astype(BF16)


def _mix_prompt(x2d, w_in_b, cos_t, sin_t, avg, g_row, b_row, wsp_b, bsp_full, batch, seq, bw, aw, tt):
    n, d = x2d.shape
    tiles_per_seq = seq // tt
    grid = (n // tt,)
    full = lambda shape: pl.BlockSpec(shape, lambda i: (0,) * len(shape))
    tok = lambda w: pl.BlockSpec((tt, w), lambda i: (i, 0))
    return pl.pallas_call(
        functools.partial(_mix_prompt_kernel, bw=bw, aw=aw),
        grid=grid,
        in_specs=[tok(d), full(w_in_b.shape),
                  pl.BlockSpec((tt, LANES), lambda i: (i % tiles_per_seq, 0)),
                  pl.BlockSpec((tt, LANES), lambda i: (i % tiles_per_seq, 0)),
                  full(avg.shape), full(g_row.shape), full(b_row.shape), full(wsp_b.shape), full(bsp_full.shape)],
        out_specs=[tok(bw), tok(bw), tok(bw),
                   pl.BlockSpec((1, bw, tt), lambda i: (i // tiles_per_seq, 0, i % tiles_per_seq)),
                   tok(aw)],
        out_shape=[jax.ShapeDtypeStruct((n, bw), F32), jax.ShapeDtypeStruct((n, bw), F32),
                   jax.ShapeDtypeStruct((n, bw), F32), jax.ShapeDtypeStruct((batch, bw, seq), BF16),
                   jax.ShapeDtypeStruct((n, aw), BF16)],
        compiler_params=_cparams(("arbitrary",)),
        name="mix_prompt",
    )(x2d, w_in_b, cos_t, sin_t, avg, g_row, b_row, wsp_b, bsp_full)


def _mix_sample(x2d, w_in_b, cos_t, sin_t, avg, g_row, b_row, w00_row, b0_row, bw, aw):
    n, d = x2d.shape
    full = lambda shape: pl.BlockSpec(shape, lambda i: (0,) * len(shape))
    ins = (x2d, w_in_b, cos_t, sin_t, avg, g_row, b_row, w00_row, b0_row)
    return pl.pallas_call(
        functools.partial(_mix_sample_kernel, bw=bw, aw=aw),
        grid=(1,),
        in_specs=[full(a.shape) for a in ins],
        out_specs=[full((n, bw))] * 3 + [full((n, aw))] * 2,
        out_shape=[jax.ShapeDtypeStruct((n, bw), F32)] * 3
                  + [jax.ShapeDtypeStruct((n, aw), F32), jax.ShapeDtypeStruct((n, aw), BF16)],
        compiler_params=_cparams(("arbitrary",)),
        name="mix_sample",
    )(*ins)


KV_UNROLL = 4
MASK_BIAS = -30000.0


def _top_rows(sc, rows, k):
    sel = jnp.zeros(sc.shape, F32)
    big = float(sc.shape[0])
    for _ in range(k):
        m = jnp.max(sc, axis=0, keepdims=True)
        first = jnp.min(jnp.where(sc == m, rows, big), axis=0, keepdims=True)
        hit = rows == first
        sel = jnp.where(hit & (m > -jnp.inf), 1.0, sel)
        sc = jnp.where(hit, -jnp.inf, sc)
    return sel


def _moba_prompt_kernel(q_ref, k_ref, vt_ref, o_ref, kmean_scr, *, nblocks):
    i = pl.program_id(2)
    tag_rows = HEAD_DIM

    @pl.when(i == 0)
    def _():
        kmean_scr[...] = jnp.zeros(kmean_scr.shape, F32)
        for nb in range(nblocks):
            kmean_scr[nb:nb + 1, :] = jnp.sum(k_ref[0, nb * BLOCK:(nb + 1) * BLOCK, :], axis=0, keepdims=True) * (1.0 / BLOCK)

    lane = lax.broadcasted_iota(jnp.int32, (1, LANES), 1)
    q2 = q_ref[...]
    kmh, kml = _split(kmean_scr[...])
    blk = lax.broadcasted_iota(jnp.int32, (tag_rows, BLOCK), 0).astype(F32)
    zeros = jnp.zeros((tag_rows, BLOCK), F32)
    mine, tag_base, qa = [], [], []
    for h in range(2):
        mine.append((lane // HEAD_DIM) == h)
        tag_base.append(HEAD_DIM * (1 - h))
        qh, ql = _split(jnp.where(mine[h], q2, 0.0))
        sc = _nt(kmh, qh) + _nt(kmh, ql) + _nt(kml, qh)
        sc = jnp.where(blk < i.astype(F32), sc, -jnp.inf)
        bias_t = jnp.where(_top_rows(sc, blk, TOPK_BLOCKS) > 0.0, 0.0, MASK_BIAS)
        bias = jnp.concatenate([zeros, bias_t] if h == 0 else [bias_t, zeros], axis=0).T
        qa.append(jnp.where(mine[h], q2 * (1.0 / math.sqrt(HEAD_DIM)), bias).astype(BF16))

    key_row = lax.broadcasted_iota(jnp.int32, (BLOCK, BLOCK), 0)
    qry_col = lax.broadcasted_iota(jnp.int32, (BLOCK, BLOCK), 1)

    def step(n0, nblk, carry, own):
        start = pl.multiple_of(n0 * BLOCK, BLOCK)
        kbs = [k_ref[0, pl.ds(start + u * BLOCK, BLOCK), :].astype(BF16) for u in range(nblk)]
        out = []
        for h in range(2):
            m, l, acc = carry[h]
            ss = []
            for u in range(nblk):
                if own:
                    s = jnp.where(key_row <= qry_col, _nt(jnp.where(mine[h], kbs[u], jnp.zeros((), BF16)), qa[h]), NEG)
                else:
                    tag = (lane == (tag_base[h] + n0 + u)).astype(BF16)
                    s = _nt(jnp.where(mine[h], kbs[u], tag), qa[h])
                ss.append(s)
            mc = ss[0]
            for s in ss[1:]:
                mc = jnp.maximum(mc, s)
            m_new = jnp.maximum(m, jnp.max(mc, axis=0, keepdims=True))
            alpha = jnp.exp(m - m_new)
            ps = [jnp.exp(s - m_new) for s in ss]
            psum = ps[0]
            for p in ps[1:]:
                psum = psum + p
            pcat = jnp.concatenate([p.astype(BF16) for p in ps], axis=0) if nblk > 1 else ps[0].astype(BF16)
            vtb = vt_ref[0, h * HEAD_DIM:(h + 1) * HEAD_DIM, pl.ds(start, nblk * BLOCK)]
            out.append((m_new, alpha * l + jnp.sum(psum, axis=0, keepdims=True), alpha * acc + _nn(vtb, pcat)))
        return tuple(out)

    init = tuple((jnp.full((1, BLOCK), NEG, F32), jnp.zeros((1, BLOCK), F32), jnp.zeros((HEAD_DIM, BLOCK), F32))
                 for _ in range(2))
    n_full = lax.div(i, KV_UNROLL)
    carry = lax.fori_loop(0, n_full, lambda g, c: step(g * KV_UNROLL, KV_UNROLL, c, False), init)
    carry = lax.fori_loop(n_full * KV_UNROLL, i, lambda j, c: step(j, 1, c, False), carry)
    carry = step(i, 1, carry, True)
    o_ref[...] = jnp.concatenate([acc / l for _, l, acc in carry], axis=0).T.astype(BF16)


def _moba_prompt(q, k, vt, batch, seq, bw):
    n = q.shape[0]
    nblocks = seq // BLOCK
    assert nblocks <= HEAD_DIM
    pairs = bw // LANES
    k3 = k.reshape(batch, seq, bw)
    return pl.pallas_call(
        functools.partial(_moba_prompt_kernel, nblocks=nblocks),
        grid=(batch, pairs, nblocks),
        in_specs=[pl.BlockSpec((BLOCK, LANES), lambda b, p, i: (b * nblocks + i, p)),
                  pl.BlockSpec((1, seq, LANES), lambda b, p, i: (b, 0, p)),
                  pl.BlockSpec((1, LANES, seq), lambda b, p, i: (b, p, 0))],
        out_specs=pl.BlockSpec((BLOCK, LANES), lambda b, p, i: (b * nblocks + i, p)),
        out_shape=jax.ShapeDtypeStruct((n, bw), BF16),
        scratch_shapes=[pltpu.VMEM((HEAD_DIM, LANES), F32)],
        compiler_params=_cparams(("arbitrary", "arbitrary", "arbitrary")),
        name="moba_prompt",
    )(q, k3, vt)


PAGES_PER_STEP = 16
PAGES_PER_BLOCK = BLOCK // PAGE


def _pages_kernel(pt_ref, q_ref, *rest, nblocks):
    page_refs = rest[:PAGES_PER_STEP]
    sel_ref = rest[PAGES_PER_STEP]
    sums_scr = rest[PAGES_PER_STEP + 1]
    g = pl.program_id(1)
    bps = PAGES_PER_STEP // PAGES_PER_BLOCK
    for b in range(bps):
        acc = jnp.sum(page_refs[b * PAGES_PER_BLOCK][0, 0], axis=0)
        for r in range(1, PAGES_PER_BLOCK):
            acc = acc + jnp.sum(page_refs[b * PAGES_PER_BLOCK + r][0, 0], axis=0)
        sums_scr[g * bps + b] = acc

    @pl.when(g == pl.num_programs(1) - 1)
    def _():
        kmean = sums_scr[...] * (1.0 / BLOCK)
        sc = jnp.sum(kmean * q_ref[0][None], axis=-1, keepdims=True)
        rows = lax.broadcasted_iota(jnp.int32, sc.shape, 0).astype(F32)
        picks = []
        for _ in range(TOPK_BLOCKS):
            m = jnp.max(sc, axis=0, keepdims=True)
            first = jnp.min(jnp.where(sc == m, rows, float(nblocks)), axis=0, keepdims=True)
            picks.append(first)
            sc = jnp.where(rows == first, -jnp.inf, sc)
        sel_ref[0] = jnp.concatenate(picks, axis=0).astype(jnp.int32)


def _sample_block_choice(cache_k, layer, pt_flat, q_s, n_seq, n_pages, heads):
    nblocks = n_pages // PAGES_PER_BLOCK
    assert nblocks >= TOPK_BLOCKS
    steps = n_pages // PAGES_PER_STEP

    def page_spec(r):
        return pl.BlockSpec((1, 1, PAGE, heads, HEAD_DIM),
                            lambda s, g, pt: (layer, pt[s * n_pages + g * PAGES_PER_STEP + r], 0, 0, 0))

    grid_spec = pltpu.PrefetchScalarGridSpec(
        num_scalar_prefetch=1,
        grid=(n_seq, steps),
        in_specs=[pl.BlockSpec((1, heads, HEAD_DIM), lambda s, g, pt: (s, 0, 0))]
                 + [page_spec(r) for r in range(PAGES_PER_STEP)],
        out_specs=pl.BlockSpec((1, TOPK_BLOCKS, heads, 1), lambda s, g, pt: (s, 0, 0, 0)),
        scratch_shapes=[pltpu.VMEM((nblocks, heads, HEAD_DIM), F32)],
    )
    return pl.pallas_call(
        functools.partial(_pages_kernel, nblocks=nblocks),
        grid_spec=grid_spec,
        out_shape=jax.ShapeDtypeStruct((n_seq, TOPK_BLOCKS, heads, 1), jnp.int32),
        compiler_params=_cparams(("arbitrary", "arbitrary")),
        name="sample_block_choice",
    )(pt_flat, q_s.reshape(n_seq, heads, HEAD_DIM), *([cache_k] * PAGES_PER_STEP))


N_FETCH = TOPK_BLOCKS * PAGES_PER_BLOCK


def _sattn_kernel(pt_ref, sel_ref, q_ref, kn_ref, vn_ref, ck_hbm, cv_hbm, o_ref, kbuf, vbuf, sem,
                  *, layer, n_pages, heads):
    s = pl.program_id(0)
    n = pl.num_programs(0)

    def copies(seq, slot):
        out = []
        for h in range(heads):
            for r in range(N_FETCH):
                pick, pg = divmod(r, PAGES_PER_BLOCK)
                blk = sel_ref[(seq * TOPK_BLOCKS + pick) * heads + h]
                page = pt_ref[seq * n_pages + blk * PAGES_PER_BLOCK + pg]
                dst = pl.ds(r * PAGE, PAGE)
                out.append(pltpu.make_async_copy(ck_hbm.at[layer, page, :, h, :], kbuf.at[slot, h, dst, :], sem.at[0, slot]))
                out.append(pltpu.make_async_copy(cv_hbm.at[layer, page, :, h, :], vbuf.at[slot, h, dst, :], sem.at[1, slot]))
        return out

    slot = lax.rem(s, 2)

    @pl.when(s == 0)
    def _():
        for c in copies(0, 0):
            c.start()

    @pl.when(s + 1 < n)
    def _():
        for c in copies(s + 1, 1 - slot):
            c.start()

    for c in copies(s, slot):
        c.wait()

    q3 = q_ref[0]
    kn3 = kn_ref[0]
    vn3 = vn_ref[0]
    outs = []
    for h in range(heads):
        qh = q3[h:h + 1, :] * (1.0 / math.sqrt(HEAD_DIM))
        qb = qh.astype(BF16)
        sc = _nt(jnp.broadcast_to(qb, (8, HEAD_DIM)), kbuf[slot, h].astype(BF16))[0:1, :]
        s_own = jnp.sum(qb.astype(F32) * kn3[h:h + 1, :].astype(BF16).astype(F32), axis=1, keepdims=True)
        m = jnp.maximum(jnp.max(sc, axis=1, keepdims=True), s_own)
        p = jnp.exp(sc - m)
        p_own = jnp.exp(s_own - m)
        l = jnp.sum(p, axis=1, keepdims=True) + p_own
        pv = _nn(jnp.broadcast_to(p, (8, N_FETCH * PAGE)).astype(BF16), vbuf[slot, h].astype(BF16))[0:1, :]
        outs.append((pv + p_own * vn3[h:h + 1, :].astype(BF16).astype(F32)) / l)
    o_ref[0] = jnp.concatenate(outs, axis=0).astype(BF16)


def _sample_attention(cache_k, cache_v, layer, pt_flat, sel_flat, q_s, k_s, v_s, n_seq, n_pages, heads):
    row = pl.BlockSpec((1, heads, HEAD_DIM), lambda s, pt, sel: (s, 0, 0))
    anyspec = pl.BlockSpec(memory_space=pl.ANY)
    grid_spec = pltpu.PrefetchScalarGridSpec(
        num_scalar_prefetch=2,
        grid=(n_seq,),
        in_specs=[row, row, row, anyspec, anyspec],
        out_specs=row,
        scratch_shapes=[pltpu.VMEM((2, heads, N_FETCH * PAGE, HEAD_DIM), F32),
                        pltpu.VMEM((2, heads, N_FETCH * PAGE, HEAD_DIM), F32),
                        pltpu.SemaphoreType.DMA((2, 2))],
    )
    r3 = lambda a: a.reshape(n_seq, heads, HEAD_DIM)
    return pl.pallas_call(
        functools.partial(_sattn_kernel, layer=layer, n_pages=n_pages, heads=heads),
        grid_spec=grid_spec,
        out_shape=jax.ShapeDtypeStruct((n_seq, heads, HEAD_DIM), BF16),
        compiler_params=_cparams(("arbitrary",)),
        name="sample_attention",
    )(pt_flat, sel_flat, r3(q_s), r3(k_s), r3(v_s), cache_k, cache_v).reshape(n_seq, heads * HEAD_DIM)


def _cand_layout():
    a_of, b_of, valid = [], [], []
    def add(a, b):
        a_of.append(a)
        b_of.append(b)
        valid.append((a + 1) * (b + 1) <= P_TOPK)
    for b in range(16):
        add(0, b)
    for a in range(1, 8):
        for b in range(8):
            add(a, b)
    for a in range(8, 16):
        add(a, 0)
    return a_of, b_of, valid


_CAND_A, _CAND_B, _CAND_VALID = _cand_layout()
N_CAND = len(_CAND_A)
_BIG_IDX = float(1 << 20)


def _take_max(s, idx):
    m = jnp.max(s, axis=0, keepdims=True)
    first = jnp.min(jnp.where(s == m, idx, _BIG_IDX), axis=0, keepdims=True)
    return m, idx == first


def _top16_ranks(s, idx):
    rank = jnp.full(s.shape, float(P_TOPK), F32)
    ts = []
    for r in range(P_TOPK):
        m, hit = _take_max(s, idx)
        rank = jnp.where(hit, float(r), rank)
        s = jnp.where(hit, -jnp.inf, s)
        ts.append(m)
    return rank, jnp.concatenate(ts, axis=0)


def _post_kernel(attn_ref, ch_ref, x_ref, woa_ref, wob_ref, g1_ref, b1_ref, wq_ref, sk_ref,
                 ca_ref, cidx_ref, cval_ref,
                 x1t_ref, r1_ref, p1_ref, cut_ref, p0_ref, s_scr, *, alpha):
    z = _nn(attn_ref[...], woa_ref[...]) + _nn(ch_ref[...], wob_ref[...]) + alpha * x_ref[...]
    mu = jnp.mean(z, axis=1, keepdims=True)
    cen = z - mu
    var = jnp.mean(cen * cen, axis=1, keepdims=True)
    x1 = cen * lax.rsqrt(var + LN_EPS) * g1_ref[...] + b1_ref[...]
    x1t_ref[...] = x1.T
    xb = x1.astype(BF16)
    t = x1.shape[0]

    def per_head(h, _):
        for c in range(2):
            rows = pl.ds(pl.multiple_of((h * 2 + c) * KEY_DIM, KEY_DIM), KEY_DIM)
            qt = _nt(wq_ref[rows, :], xb)
            s_scr[c] = _nn(sk_ref[h * 2 + c], qt.astype(BF16))

        def per_tile(lt, _):
            cols = pl.ds(pl.multiple_of(lt * LANES, LANES), LANES)
            key_row = lax.broadcasted_iota(jnp.int32, (N_KEYS, LANES), 0).astype(F32)
            s0 = s_scr[0, :, cols]
            s1 = s_scr[1, :, cols]
            rank0, ts0 = _top16_ranks(s0, key_row)
            rank1, ts1 = _top16_ranks(s1, key_row)
            pieces = [ts0[0:1, :] + ts1]
            for a in range(1, 8):
                pieces.append(ts0[a:a + 1, :] + ts1[0:8, :])
            pieces.append(ts0[8:16, :] + ts1[0:1, :])
            cand = jnp.where(cval_ref[...] > 0.0, jnp.concatenate(pieces, axis=0), -jnp.inf)
            cidx = cidx_ref[...]
            ca = ca_ref[...]
            a_iota = lax.broadcasted_iota(jnp.int32, (P_TOPK, LANES), 0).astype(F32)
            cut = jnp.zeros((P_TOPK, LANES), F32)
            mass = jnp.zeros((1, LANES), F32)
            for r in range(P_TOPK):
                m, hit = _take_max(cand, cidx)
                cand = jnp.where(hit, -jnp.inf, cand)
                a_pick = jnp.max(jnp.where(hit, ca, -1.0), axis=0, keepdims=True)
                cut = cut + jnp.where(a_iota == a_pick, 1.0, 0.0)
                m0 = m if r == 0 else m0
                mass = mass + jnp.exp(m - m0)
            cutd = jnp.zeros((N_KEYS, LANES), F32)
            for a in range(P_TOPK):
                cutd = jnp.where(rank0 == float(a), cut[a:a + 1, :], cutd)
            p0 = jnp.where(rank0 < float(P_TOPK), jnp.exp(s0 - ts0[0:1, :]), 0.0)
            p1 = jnp.where(rank1 < float(P_TOPK), jnp.exp(s1 - ts1[0:1, :]), 0.0) / mass
            r1_ref[h, :, cols] = rank1.astype(BF16)
            p1_ref[h, :, cols] = p1.astype(BF16)
            cut_ref[h, :, cols] = cutd
            p0_ref[h, :, cols] = p0
            return 0

        lax.fori_loop(0, t // LANES, per_tile, 0)
        return 0

    lax.fori_loop(0, P_HEADS, per_head, 0)


def _post(attn, ch, x2d, woa, wob, g1, b1, wq_b, sk_b, alpha, tb):
    n, d = x2d.shape
    bw, aw = attn.shape[1], ch.shape[1]
    assert tb % LANES == 0
    wide = lambda v: jnp.broadcast_to(jnp.asarray(v, F32).reshape(N_CAND, 1), (N_CAND, LANES))
    ca = wide(_CAND_A)
    cidx = wide([a * P_TOPK + b for a, b in zip(_CAND_A, _CAND_B)])
    cval = wide(_CAND_VALID)
    full = lambda a: pl.BlockSpec(a.shape, lambda i: (0,) * a.ndim)
    tok = lambda w: pl.BlockSpec((tb, w), lambda i: (i, 0))
    gate_spec = pl.BlockSpec((P_HEADS, N_KEYS, tb), lambda i: (0, 0, i))
    return pl.pallas_call(
        functools.partial(_post_kernel, alpha=alpha),
        grid=(n // tb,),
        in_specs=[tok(bw), tok(aw), tok(d), full(woa), full(wob), full(g1), full(b1), full(wq_b), full(sk_b),
                  full(ca), full(cidx), full(cval)],
        out_specs=[pl.BlockSpec((d, tb), lambda i: (0, i)), gate_spec, gate_spec, gate_spec, gate_spec],
        out_shape=[jax.ShapeDtypeStruct((d, n), F32),
                   jax.ShapeDtypeStruct((P_HEADS, N_KEYS, n), BF16), jax.ShapeDtypeStruct((P_HEADS, N_KEYS, n), BF16),
                   jax.ShapeDtypeStruct((P_HEADS, N_KEYS, n), F32), jax.ShapeDtypeStruct((P_HEADS, N_KEYS, n), F32)],
        scratch_shapes=[pltpu.VMEM((2, N_KEYS, tb), F32)],
        compiler_params=_cparams(("arbitrary",)),
        name="post",
    )(attn, ch, x2d, woa, wob, g1, b1, wq_b, sk_b, ca, cidx, cval)


EXPERT_TILE = 1024


def _peer_kernel(x1t_ref, r1_ref, p1_ref, cut_ref, p0_ref, u_ref, vt_ref, p_ref, g2_ref, b2_ref,
                 gwt_ref, gb_ref, pwt_ref, y_ref, acc_scr, xb_scr, g_scr, *, alpha):
    e = pl.program_id(1)

    @pl.when(e == 0)
    def _():
        acc_scr[...] = jnp.zeros(acc_scr.shape, F32)
        xb_scr[...] = x1t_ref[...].astype(BF16)

    ht = _nn(u_ref[...], xb_scr[...])
    for ib in range(EXPERT_TILE // N_KEYS):
        w = None
        for h in range(P_HEADS):
            cut_row = cut_ref[h, ib:ib + 1, :].astype(BF16)
            p0_row = p0_ref[h, ib:ib + 1, :].astype(BF16)
            term = jnp.where(r1_ref[h] < cut_row, p1_ref[h], jnp.zeros((), BF16)) * p0_row
            w = term if w is None else w + term
        rows = slice(ib * N_KEYS, (ib + 1) * N_KEYS)
        g_scr[rows, :] = (_gelu(ht[rows, :]) * w.astype(F32)).astype(BF16)
    acc_scr[...] += _nn(vt_ref[...], g_scr[...])

    @pl.when(e == pl.num_programs(1) - 1)
    def _():
        z = alpha * x1t_ref[...] + acc_scr[...]
        mu = jnp.mean(z, axis=0, keepdims=True)
        cen = z - mu
        var = jnp.mean(cen * cen, axis=0, keepdims=True)
        x2 = cen * lax.rsqrt(var + LN_EPS) * g2_ref[...] + b2_ref[...]
        gate = jax.nn.sigmoid(_nn(gwt_ref[...], x2.astype(BF16)) + gb_ref[...])
        ple = _nt(pwt_ref[...], p_ref[...].astype(BF16))
        y_ref[...] = (x2 + gate * ple).T


def _peer(x1t, r1, p1, cut, p0, u_b, vt_b, p2d, g2c, b2c, gwt, gbc, pwt, alpha, tb):
    d, n = x1t.shape
    n_exp = u_b.shape[0]
    kpt = EXPERT_TILE // N_KEYS
    tcol = lambda rows: pl.BlockSpec((rows, tb), lambda t, e: (0, t))
    full = lambda a: pl.BlockSpec(a.shape, lambda t, e: (0,) * a.ndim)
    return pl.pallas_call(
        functools.partial(_peer_kernel, alpha=alpha),
        grid=(n // tb, n_exp // EXPERT_TILE),
        in_specs=[tcol(d),
                  pl.BlockSpec((P_HEADS, N_KEYS, tb), lambda t, e: (0, 0, t)),
                  pl.BlockSpec((P_HEADS, N_KEYS, tb), lambda t, e: (0, 0, t)),
                  pl.BlockSpec((P_HEADS, kpt, tb), lambda t, e: (0, e, t)),
                  pl.BlockSpec((P_HEADS, kpt, tb), lambda t, e: (0, e, t)),
                  pl.BlockSpec((EXPERT_TILE, d), lambda t, e: (e, 0)),
                  pl.BlockSpec((d, EXPERT_TILE), lambda t, e: (0, e)),
                  pl.BlockSpec((tb, p2d.shape[1]), lambda t, e: (t, 0)),
                  full(g2c), full(b2c), full(gwt), full(gbc), full(pwt)],
        out_specs=pl.BlockSpec((tb, d), lambda t, e: (t, 0)),
        out_shape=jax.ShapeDtypeStruct((n, d), F32),
        scratch_shapes=[pltpu.VMEM((d, tb), F32), pltpu.VMEM((d, tb), BF16), pltpu.VMEM((EXPERT_TILE, tb), BF16)],
        compiler_params=_cparams(("arbitrary", "arbitrary")),
        name="peer",
    )(x1t, r1, p1, cut, p0, u_b, vt_b, p2d, g2c, b2c, gwt, gbc, pwt)


def _finish(x2d, attn, ch, p2d, prm, alpha, tb):
    x1t, r1, p1, cut, p0 = _post(attn, ch, x2d, prm["woa"], prm["wob"], prm["g1"], prm["b1"], prm["wq_b"],
                                 prm["sk_b"], alpha, tb)
    return _peer(x1t, r1, p1, cut, p0, prm["u_b"], prm["vt_b"], p2d, prm["g2c"], prm["b2c"], prm["gwt"],
                 prm["gbc"], prm["pwt"], alpha, tb)


def kernel(x_prompt, x_sample, cache_k, cache_v, page_table, p_prompt, p_sample, w_in, vnorm_g, vnorm_b, w_spatial, b_spatial, w_out, ln1_g, ln1_b, peer_wq, peer_subkeys, peer_u, peer_v, ln2_g, ln2_b, ple_w, ple_gate_w, ple_gate_b):
    batch, seq, d = x_prompt.shape
    n_seq, t_new, _ = x_sample.shape
    depth = w_in.shape[0]
    n_pages = page_table.shape[1]
    past = n_pages * PAGE
    h_a = vnorm_g.shape[1]
    aw = h_a * HEAD_DIM
    bw = (w_in.shape[2] - 2 * aw) // 3
    h_b = bw // HEAD_DIM
    assert t_new == 1 and seq % BLOCK == 0 and past % BLOCK == 0 and n_pages % PAGES_PER_STEP == 0
    assert bw % LANES == 0 and aw % LANES == 0
    assert cache_k.shape[2:] == (PAGE, h_b, HEAD_DIM)
    alpha = (2.0 * depth) ** 0.25

    tt = 512 if seq % 512 == 0 else BLOCK
    tb_p = 512 if (batch * seq) % 512 == 0 else BLOCK

    cos_p, sin_p = _rope_tables(jnp.arange(seq))
    cos_s, sin_s = _rope_tables(jnp.full((n_seq,), past, jnp.int32))
    avg = _head_avg_matrix(aw)
    tril = jnp.tril(jnp.ones((CHUNK, CHUNK), F32))
    pt_flat = page_table.reshape(-1).astype(jnp.int32)

    xp = x_prompt.reshape(batch * seq, d)
    xs = x_sample.reshape(n_seq, d)
    kp_l, vp_l, ks_l, vs_l, cs_l = [], [], [], [], []
    for i in range(depth):
        w_in_b = w_in[i].astype(BF16)
        g_row = vnorm_g[i].reshape(1, aw)
        b_row = vnorm_b[i].reshape(1, aw)
        wsp_b = (w_spatial[i] * tril).astype(BF16)
        bsp_full = jnp.repeat(b_spatial[i].T, HEAD_DIM, axis=1)
        w00_row = jnp.repeat(w_spatial[i][:, 0, 0], HEAD_DIM).reshape(1, aw)
        b0_row = jnp.repeat(b_spatial[i][:, 0], HEAD_DIM).reshape(1, aw)
        prm = dict(
            woa=w_out[i][:bw].astype(BF16), wob=w_out[i][bw:].astype(BF16),
            g1=ln1_g[i].reshape(1, d), b1=ln1_b[i].reshape(1, d),
            wq_b=peer_wq[i].T.astype(BF16),
            sk_b=peer_subkeys[i].reshape(P_HEADS * 2, N_KEYS, KEY_DIM).astype(BF16),
            u_b=peer_u[i].astype(BF16), vt_b=peer_v[i].T.astype(BF16),
            g2c=ln2_g[i].reshape(d, 1), b2c=ln2_b[i].reshape(d, 1),
            gwt=ple_gate_w[i].T.astype(BF16), gbc=ple_gate_b[i].reshape(d, 1), pwt=ple_w[i].T.astype(BF16))

        qp, kp, vp, vtp, chp = _mix_prompt(xp, w_in_b, cos_p, sin_p, avg, g_row, b_row, wsp_b, bsp_full,
                                           batch, seq, bw, aw, tt)
        attn_p = _moba_prompt(qp, kp, vtp, batch, seq, bw)
        qs, kn, vn, cvs, chs = _mix_sample(xs, w_in_b, cos_s, sin_s, avg, g_row, b_row, w00_row, b0_row, bw, aw)
        sel = _sample_block_choice(cache_k, i, pt_flat, qs, n_seq, n_pages, h_b)
        attn_s = _sample_attention(cache_k, cache_v, i, pt_flat, sel.reshape(-1), qs, kn, vn, n_seq, n_pages, h_b)

        xp = _finish(xp, attn_p, chp, p_prompt[i].reshape(batch * seq, -1), prm, alpha, tb_p)
        xs = _finish(xs, attn_s, chs, p_sample[i].reshape(n_seq, -1), prm, alpha, n_seq)
        kp_l.append(kp.reshape(batch, seq, h_b, HEAD_DIM))
        vp_l.append(vp.reshape(batch, seq, h_b, HEAD_DIM))
        ks_l.append(kn.reshape(n_seq, 1, h_b, HEAD_DIM))
        vs_l.append(vn.reshape(n_seq, 1, h_b, HEAD_DIM))
        cs_l.append(cvs.reshape(n_seq, 1, h_a, HEAD_DIM))
    return (xp.reshape(batch, seq, d), xs.reshape(n_seq, 1, d), jnp.stack(kp_l), jnp.stack(vp_l),
            jnp.stack(ks_l), jnp.stack(vs_l), jnp.stack(cs_l))
```

```python
import functools
import math

import jax
import jax.numpy as jnp
from jax import lax
from jax.experimental import pallas as pl
from jax.experimental.pallas import tpu as pltpu

HEAD_DIM = 64
BLOCK = 256
PAGE = 128
CHUNK = 128
TOPK_BLOCKS = 3
N_KEYS = 128
P_HEADS = 8
KEY_DIM = 128
P_TOPK = 16
LN_EPS = 1e-5
ROPE_THETA = 10000.0
LANES = 128
NEG = -1e30

VMEM_LIMIT = 56 * 1024 * 1024

F32 = jnp.float32
BF16 = jnp.bfloat16


def _nt(a, b):
    return lax.dot_general(a, b, (((1,), (1,)), ((), ())), preferred_element_type=F32)


def _nn(a, b):
    return jnp.dot(a, b, preferred_element_type=F32)


def _split(a):
    hi = a.astype(BF16)
    lo = (a - hi.astype(F32)).astype(BF16)
    return hi, lo


def _gelu(x):
    return 0.5 * x * (1.0 + lax.erf(x * (1.0 / math.sqrt(2.0))))


def _cparams(sem):
    return pltpu.CompilerParams(dimension_semantics=sem, vmem_limit_bytes=VMEM_LIMIT)


def _rope_cols(a, cos, sin_signed):
    outs = []
    lane = lax.broadcasted_iota(jnp.int32, (1, LANES), 1)
    first_half = (lane % HEAD_DIM) < (HEAD_DIM // 2)
    for c in range(a.shape[1] // LANES):
        ac = a[:, c * LANES:(c + 1) * LANES]
        partner = jnp.where(first_half, pltpu.roll(ac, LANES - HEAD_DIM // 2, 1), pltpu.roll(ac, HEAD_DIM // 2, 1))
        outs.append(ac * cos + partner * sin_signed)
    return jnp.concatenate(outs, axis=1)


def _mix_core(x_ref, w_ref, cos_ref, sin_ref, avg_ref, g_ref, b_ref, bw, aw):
    z = _nn(x_ref[...].astype(BF16), w_ref[...])
    cos = cos_ref[...]
    sin = sin_ref[...]
    q = _rope_cols(z[:, 0:bw], cos, sin)
    k = _rope_cols(z[:, bw:2 * bw], cos, sin)
    v = z[:, 2 * bw:3 * bw]
    u = _gelu(z[:, 3 * bw:3 * bw + aw])
    gv = _gelu(z[:, 3 * bw + aw:3 * bw + 2 * aw])
    avg = avg_ref[...]
    gh, gl = _split(gv)
    mu = _nn(gh, avg) + _nn(gl, avg)
    cen = gv - mu
    sh, sl = _split(cen * cen)
    var = _nn(sh, avg) + _nn(sl, avg)
    cvn = cen * lax.rsqrt(var + LN_EPS) * g_ref[...] + b_ref[...]
    return q, k, v, u, cvn


def _mix_prompt_kernel(x_ref, w_ref, cos_ref, sin_ref, avg_ref, g_ref, b_ref, wsp_ref, bsp_ref,
                       q_ref, k_ref, v_ref, vt_ref, ch_ref, *, bw, aw):
    q, k, v, u, cvn = _mix_core(x_ref, w_ref, cos_ref, sin_ref, avg_ref, g_ref, b_ref, bw, aw)
    q_ref[...] = q
    k_ref[...] = k
    v_ref[...] = v
    vt_ref[0] = v.T.astype(BF16)
    lane = lax.broadcasted_iota(jnp.int32, (1, LANES), 1)
    low = lane < HEAD_DIM
    cvb = cvn.astype(BF16)
    for c in range(cvn.shape[0] // CHUNK):
        rows = slice(c * CHUNK, (c + 1) * CHUNK)
        for p in range(aw // LANES):
            cols = slice(p * LANES, (p + 1) * LANES)
            cp = cvb[rows, cols]
            mixed = jnp.where(low, _nn(wsp_ref[2 * p], cp), _nn(wsp_ref[2 * p + 1], cp)) + bsp_ref[:, cols]
            ch_ref[rows, cols] = (u[rows, cols] * mixed).astype(BF16)


def _mix_sample_kernel(x_ref, w_ref, cos_ref, sin_ref, avg_ref, g_ref, b_ref, w00_ref, b0_ref,
                       q_ref, k_ref, v_ref, cv_ref, ch_ref, *, bw, aw):
    q, k, v, u, cvn = _mix_core(x_ref, w_ref, cos_ref, sin_ref, avg_ref, g_ref, b_ref, bw, aw)
    q_ref[...] = q
    k_ref[...] = k
    v_ref[...] = v
    cv_ref[...] = cvn
    ch_ref[...] = (u * (cvn * w00_ref[...] + b0_ref[...])).astype(BF16)


def _rope_tables(pos):
    half = HEAD_DIM // 2
    inv_freq = ROPE_THETA ** (-jnp.arange(half, dtype=F32) / half)
    ang = pos.astype(F32)[:, None] * inv_freq[None, :]
    cos, sin = jnp.cos(ang), jnp.sin(ang)
    cos_t = jnp.concatenate([cos, cos, cos, cos], axis=1)
    sin_t = jnp.concatenate([-sin, sin, -sin, sin], axis=1)
    return cos_t, sin_t


def _head_avg_matrix(width):
    h = jnp.arange(width) // HEAD_DIM
    return jnp.where(h[:, None] == h[None, :], 1.0 / HEAD_DIM, 0.0).astype(BF16)


def _mix_prompt(x2d, w_in_b, cos_t, sin_t, avg, g_row, b_row, wsp_b, bsp_full, batch, seq, bw, aw, tt):
    n, d = x2d.shape
    tiles_per_seq = seq // tt
    grid = (n // tt,)
    full = lambda shape: pl.BlockSpec(shape, lambda i: (0,) * len(shape))
    tok = lambda w: pl.BlockSpec((tt, w), lambda i: (i, 0))
    return pl.pallas_call(
        functools.partial(_mix_prompt_kernel, bw=bw, aw=aw),
        grid=grid,
        in_specs=[tok(d), full(w_in_b.shape),
                  pl.BlockSpec((tt, LANES), lambda i: (i % tiles_per_seq, 0)),
                  pl.BlockSpec((tt, LANES), lambda i: (i % tiles_per_seq, 0)),
                  full(avg.shape), full(g_row.shape), full(b_row.shape), full(wsp_b.shape), full(bsp_full.shape)],
        out_specs=[tok(bw), tok(bw), tok(bw),
                   pl.BlockSpec((1, bw, tt), lambda i: (i // tiles_per_seq, 0, i % tiles_per_seq)),
                   tok(aw)],
        out_shape=[jax.ShapeDtypeStruct((n, bw), F32), jax.ShapeDtypeStruct((n, bw), F32),
                   jax.ShapeDtypeStruct((n, bw), F32), jax.ShapeDtypeStruct((batch, bw, seq), BF16),
                   jax.ShapeDtypeStruct((n, aw), BF16)],
        compiler_params=_cparams(("arbitrary",)),
        name="mix_prompt",
    )(x2d, w_in_b, cos_t, sin_t, avg, g_row, b_row, wsp_b, bsp_full)


def _mix_sample(x2d, w_in_b, cos_t, sin_t, avg, g_row, b_row, w00_row, b0_row, bw, aw):
    n, d = x2d.shape
    full = lambda shape: pl.BlockSpec(shape, lambda i: (0,) * len(shape))
    ins = (x2d, w_in_b, cos_t, sin_t, avg, g_row, b_row, w00_row, b0_row)
    return pl.pallas_call(
        functools.partial(_mix_sample_kernel, bw=bw, aw=aw),
        grid=(1,),
        in_specs=[full(a.shape) for a in ins],
        out_specs=[full((n, bw))] * 3 + [full((n, aw))] * 2,
        out_shape=[jax.ShapeDtypeStruct((n, bw), F32)] * 3
                  + [jax.ShapeDtypeStruct((n, aw), F32), jax.ShapeDtypeStruct((n, aw), BF16)],
        compiler_params=_cparams(("arbitrary",)),
        name="mix_sample",
    )(*ins)


KV_UNROLL = 4
MASK_BIAS = -30000.0


def _top_rows(sc, rows, k):
    sel = jnp.zeros(sc.shape, F32)
    big = float(sc.shape[0])
    for _ in range(k):
        m = jnp.max(sc, axis=0, keepdims=True)
        first = jnp.min(jnp.where(sc == m, rows, big), axis=0, keepdims=True)
        hit = rows == first
        sel = jnp.where(hit & (m > -jnp.inf), 1.0, sel)
        sc = jnp.where(hit, -jnp.inf, sc)
    return sel


def _moba_prompt_kernel(q_ref, k_ref, vt_ref, o_ref, kmean_scr, *, nblocks):
    i = pl.program_id(2)
    tag_rows = HEAD_DIM

    @pl.when(i == 0)
    def _():
        kmean_scr[...] = jnp.zeros(kmean_scr.shape, F32)
        for nb in range(nblocks):
            kmean_scr[nb:nb + 1, :] = jnp.sum(k_ref[0, nb * BLOCK:(nb + 1) * BLOCK, :], axis=0, keepdims=True) * (1.0 / BLOCK)

    lane = lax.broadcasted_iota(jnp.int32, (1, LANES), 1)
    q2 = q_ref[...]
    kmh, kml = _split(kmean_scr[...])
    blk = lax.broadcasted_iota(jnp.int32, (tag_rows, BLOCK), 0).astype(F32)
    zeros = jnp.zeros((tag_rows, BLOCK), F32)
    mine, tag_base, qa = [], [], []
    for h in range(2):
        mine.append((lane // HEAD_DIM) == h)
        tag_base.append(HEAD_DIM * (1 - h))
        qh, ql = _split(jnp.where(mine[h], q2, 0.0))
        sc = _nt(kmh, qh) + _nt(kmh, ql) + _nt(kml, qh)
        sc = jnp.where(blk < i.astype(F32), sc, -jnp.inf)
        bias_t = jnp.where(_top_rows(sc, blk, TOPK_BLOCKS) > 0.0, 0.0, MASK_BIAS)
        bias = jnp.concatenate([zeros, bias_t] if h == 0 else [bias_t, zeros], axis=0).T
        qa.append(jnp.where(mine[h], q2 * (1.0 / math.sqrt(HEAD_DIM)), bias).astype(BF16))

    key_row = lax.broadcasted_iota(jnp.int32, (BLOCK, BLOCK), 0)
    qry_col = lax.broadcasted_iota(jnp.int32, (BLOCK, BLOCK), 1)

    def step(n0, nblk, carry, own):
        start = pl.multiple_of(n0 * BLOCK, BLOCK)
        kbs = [k_ref[0, pl.ds(start + u * BLOCK, BLOCK), :].astype(BF16) for u in range(nblk)]
        out = []
        for h in range(2):
            m, l, acc = carry[h]
            ss = []
            for u in range(nblk):
                if own:
                    s = jnp.where(key_row <= qry_col, _nt(jnp.where(mine[h], kbs[u], jnp.zeros((), BF16)), qa[h]), NEG)
                else:
                    tag = (lane == (tag_base[h] + n0 + u)).astype(BF16)
                    s = _nt(jnp.where(mine[h], kbs[u], tag), qa[h])
                ss.append(s)
            mc = ss[0]
            for s in ss[1:]:
                mc = jnp.maximum(mc, s)
            m_new = jnp.maximum(m, jnp.max(mc, axis=0, keepdims=True))
            alpha = jnp.exp(m - m_new)
            ps = [jnp.exp(s - m_new) for s in ss]
            psum = ps[0]
            for p in ps[1:]:
                psum = psum + p
            pcat = jnp.concatenate([p.astype(BF16) for p in ps], axis=0) if nblk > 1 else ps[0].astype(BF16)
            vtb = vt_ref[0, h * HEAD_DIM:(h + 1) * HEAD_DIM, pl.ds(start, nblk * BLOCK)]
            out.append((m_new, alpha * l + jnp.sum(psum, axis=0, keepdims=True), alpha * acc + _nn(vtb, pcat)))
        return tuple(out)

    init = tuple((jnp.full((1, BLOCK), NEG, F32), jnp.zeros((1, BLOCK), F32), jnp.zeros((HEAD_DIM, BLOCK), F32))
                 for _ in range(2))
    n_full = lax.div(i, KV_UNROLL)
    carry = lax.fori_loop(0, n_full, lambda g, c: step(g * KV_UNROLL, KV_UNROLL, c, False), init)
    carry = lax.fori_loop(n_full * KV_UNROLL, i, lambda j, c: step(j, 1, c, False), carry)
    carry = step(i, 1, carry, True)
    o_ref[...] = jnp.concatenate([acc / l for _, l, acc in carry], axis=0).T.astype(BF16)


def _moba_prompt(q, k, vt, batch, seq, bw):
    n = q.shape[0]
    nblocks = seq // BLOCK
    assert nblocks <= HEAD_DIM
    pairs = bw // LANES
    k3 = k.reshape(batch, seq, bw)
    return pl.pallas_call(
        functools.partial(_moba_prompt_kernel, nblocks=nblocks),
        grid=(batch, pairs, nblocks),
        in_specs=[pl.BlockSpec((BLOCK, LANES), lambda b, p, i: (b * nblocks + i, p)),
                  pl.BlockSpec((1, seq, LANES), lambda b, p, i: (b, 0, p)),
                  pl.BlockSpec((1, LANES, seq), lambda b, p, i: (b, p, 0))],
        out_specs=pl.BlockSpec((BLOCK, LANES), lambda b, p, i: (b * nblocks + i, p)),
        out_shape=jax.ShapeDtypeStruct((n, bw), BF16),
        scratch_shapes=[pltpu.VMEM((HEAD_DIM, LANES), F32)],
        compiler_params=_cparams(("arbitrary", "arbitrary", "arbitrary")),
        name="moba_prompt",
    )(q, k3, vt)


PAGES_PER_STEP = 16
PAGES_PER_BLOCK = BLOCK // PAGE


def _pages_kernel(pt_ref, q_ref, *rest, nblocks):
    page_refs = rest[:PAGES_PER_STEP]
    sel_ref = rest[PAGES_PER_STEP]
    part_scr = rest[PAGES_PER_STEP + 1]
    g = pl.program_id(1)
    bps = PAGES_PER_STEP // PAGES_PER_BLOCK
    qcol = q_ref[0]
    for b in range(bps):
        acc = page_refs[b * PAGES_PER_BLOCK][0, 0]
        for r in range(1, PAGES_PER_BLOCK):
            acc = acc + page_refs[b * PAGES_PER_BLOCK + r][0, 0]
        part_scr[g * bps + b] = jnp.sum(acc * qcol, axis=1)

    @pl.when(g == pl.num_programs(1) - 1)
    def _():
        sc = jnp.sum(part_scr[...], axis=-1, keepdims=True) * (1.0 / BLOCK)
        rows = lax.broadcasted_iota(jnp.int32, sc.shape, 0).astype(F32)
        picks = []
        for _ in range(TOPK_BLOCKS):
            m = jnp.max(sc, axis=0, keepdims=True)
            first = jnp.min(jnp.where(sc == m, rows, float(nblocks)), axis=0, keepdims=True)
            picks.append(first)
            sc = jnp.where(rows == first, -jnp.inf, sc)
        sel_ref[0] = jnp.concatenate(picks, axis=0).astype(jnp.int32)


def _sample_block_choice(cache_kt, layer, pt_flat, q_s, n_seq, n_pages, heads):
    nblocks = n_pages // PAGES_PER_BLOCK
    assert nblocks >= TOPK_BLOCKS
    steps = n_pages // PAGES_PER_STEP

    def page_spec(r):
        return pl.BlockSpec((1, 1, heads, HEAD_DIM, PAGE),
                            lambda s, g, pt: (layer, pt[s * n_pages + g * PAGES_PER_STEP + r], 0, 0, 0))

    grid_spec = pltpu.PrefetchScalarGridSpec(
        num_scalar_prefetch=1,
        grid=(n_seq, steps),
        in_specs=[pl.BlockSpec((1, heads, HEAD_DIM, 1), lambda s, g, pt: (s, 0, 0, 0))]
                 + [page_spec(r) for r in range(PAGES_PER_STEP)],
        out_specs=pl.BlockSpec((1, TOPK_BLOCKS, heads, 1), lambda s, g, pt: (s, 0, 0, 0)),
        scratch_shapes=[pltpu.VMEM((nblocks, heads, PAGE), F32)],
    )
    return pl.pallas_call(
        functools.partial(_pages_kernel, nblocks=nblocks),
        grid_spec=grid_spec,
        out_shape=jax.ShapeDtypeStruct((n_seq, TOPK_BLOCKS, heads, 1), jnp.int32),
        compiler_params=_cparams(("arbitrary", "arbitrary")),
        name="sample_block_choice",
    )(pt_flat, q_s.reshape(n_seq, heads, HEAD_DIM, 1), *([cache_kt] * PAGES_PER_STEP))


N_FETCH = TOPK_BLOCKS * PAGES_PER_BLOCK


def _sattn_kernel(pt_ref, sel_ref, q_ref, kn_ref, vn_ref, ck_hbm, cv_hbm, o_ref, kbuf, vbuf, sem,
                  *, layer, n_pages, heads):
    s = pl.program_id(0)
    n = pl.num_programs(0)

    def copies(seq, slot):
        out = []
        for h in range(heads):
            for r in range(N_FETCH):
                pick, pg = divmod(r, PAGES_PER_BLOCK)
                blk = sel_ref[(seq * TOPK_BLOCKS + pick) * heads + h]
                page = pt_ref[seq * n_pages + blk * PAGES_PER_BLOCK + pg]
                dst = pl.ds(r * PAGE, PAGE)
                out.append(pltpu.make_async_copy(ck_hbm.at[layer, page, h], kbuf.at[slot, h, :, dst], sem.at[0, slot]))
                out.append(pltpu.make_async_copy(cv_hbm.at[layer, page, h], vbuf.at[slot, h, :, dst], sem.at[1, slot]))
        return out

    slot = lax.rem(s, 2)

    @pl.when(s == 0)
    def _():
        for c in copies(0, 0):
            c.start()

    @pl.when(s + 1 < n)
    def _():
        for c in copies(s + 1, 1 - slot):
            c.start()

    for c in copies(s, slot):
        c.wait()

    q3 = q_ref[0]
    kn3 = kn_ref[0]
    vn3 = vn_ref[0]
    outs = []
    for h in range(heads):
        qh = q3[h:h + 1, :] * (1.0 / math.sqrt(HEAD_DIM))
        qb = qh.astype(BF16)
        sc = _nn(jnp.broadcast_to(qb, (8, HEAD_DIM)), kbuf[slot, h].astype(BF16))[0:1, :]
        s_own = jnp.sum(qb.astype(F32) * kn3[h:h + 1, :].astype(BF16).astype(F32), axis=1, keepdims=True)
        m = jnp.maximum(jnp.max(sc, axis=1, keepdims=True), s_own)
        p = jnp.exp(sc - m)
        p_own = jnp.exp(s_own - m)
        l = jnp.sum(p, axis=1, keepdims=True) + p_own
        pv = _nt(jnp.broadcast_to(p, (8, N_FETCH * PAGE)).astype(BF16), vbuf[slot, h].astype(BF16))[0:1, :]
        outs.append((pv + p_own * vn3[h:h + 1, :].astype(BF16).astype(F32)) / l)
    o_ref[0] = jnp.concatenate(outs, axis=0).astype(BF16)


def _sample_attention(cache_k, cache_v, layer, pt_flat, sel_flat, q_s, k_s, v_s, n_seq, n_pages, heads):
    row = pl.BlockSpec((1, heads, HEAD_DIM), lambda s, pt, sel: (s, 0, 0))
    anyspec = pl.BlockSpec(memory_space=pl.ANY)
    grid_spec = pltpu.PrefetchScalarGridSpec(
        num_scalar_prefetch=2,
        grid=(n_seq,),
        in_specs=[row, row, row, anyspec, anyspec],
        out_specs=row,
        scratch_shapes=[pltpu.VMEM((2, heads, HEAD_DIM, N_FETCH * PAGE), F32),
                        pltpu.VMEM((2, heads, HEAD_DIM, N_FETCH * PAGE), F32),
                        pltpu.SemaphoreType.DMA((2, 2))],
    )
    r3 = lambda a: a.reshape(n_seq, heads, HEAD_DIM)
    return pl.pallas_call(
        functools.partial(_sattn_kernel, layer=layer, n_pages=n_pages, heads=heads),
        grid_spec=grid_spec,
        out_shape=jax.ShapeDtypeStruct((n_seq, heads, HEAD_DIM), BF16),
        compiler_params=_cparams(("arbitrary",)),
        name="sample_attention",
    )(pt_flat, sel_flat, r3(q_s), r3(k_s), r3(v_s), cache_k, cache_v).reshape(n_seq, heads * HEAD_DIM)


def _cand_layout():
    a_of, b_of, valid = [], [], []
    def add(a, b):
        a_of.append(a)
        b_of.append(b)
        valid.append((a + 1) * (b + 1) <= P_TOPK)
    for b in range(16):
        add(0, b)
    for a in range(1, 8):
        for b in range(8):
            add(a, b)
    for a in range(8, 16):
        add(a, 0)
    return a_of, b_of, valid


_CAND_A, _CAND_B, _CAND_VALID = _cand_layout()
N_CAND = len(_CAND_A)
_BIG_IDX = float(1 << 20)


def _take_max(s, idx):
    m = jnp.max(s, axis=0, keepdims=True)
    first = jnp.min(jnp.where(s == m, idx, _BIG_IDX), axis=0, keepdims=True)
    return m, idx == first


def _top16_ranks(s, idx):
    rank = jnp.full(s.shape, float(P_TOPK), F32)
    ts = []
    for r in range(P_TOPK):
        m, hit = _take_max(s, idx)
        rank = jnp.where(hit, float(r), rank)
        s = jnp.where(hit, -jnp.inf, s)
        ts.append(m)
    return rank, jnp.concatenate(ts, axis=0)


def _post_kernel(attn_ref, ch_ref, x_ref, woa_ref, wob_ref, g1_ref, b1_ref, wq_ref, sk_ref,
                 ca_ref, cidx_ref, cval_ref,
                 x1t_ref, r1_ref, p1_ref, cut_ref, p0_ref, s_scr, *, alpha):
    z = _nn(attn_ref[...], woa_ref[...]) + _nn(ch_ref[...], wob_ref[...]) + alpha * x_ref[...]
    mu = jnp.mean(z, axis=1, keepdims=True)
    cen = z - mu
    var = jnp.mean(cen * cen, axis=1, keepdims=True)
    x1 = cen * lax.rsqrt(var + LN_EPS) * g1_ref[...] + b1_ref[...]
    x1t_ref[...] = x1.T
    xb = x1.astype(BF16)
    t = x1.shape[0]

    def per_head(h, _):
        for c in range(2):
            rows = pl.ds(pl.multiple_of((h * 2 + c) * KEY_DIM, KEY_DIM), KEY_DIM)
            qt = _nt(wq_ref[rows, :], xb)
            s_scr[c] = _nn(sk_ref[h * 2 + c], qt.astype(BF16))

        def per_tile(lt, _):
            cols = pl.ds(pl.multiple_of(lt * LANES, LANES), LANES)
            key_row = lax.broadcasted_iota(jnp.int32, (N_KEYS, LANES), 0).astype(F32)
            s0 = s_scr[0, :, cols]
            s1 = s_scr[1, :, cols]
            rank0, ts0 = _top16_ranks(s0, key_row)
            rank1, ts1 = _top16_ranks(s1, key_row)
            pieces = [ts0[0:1, :] + ts1]
            for a in range(1, 8):
                pieces.append(ts0[a:a + 1, :] + ts1[0:8, :])
            pieces.append(ts0[8:16, :] + ts1[0:1, :])
            cand = jnp.where(cval_ref[...] > 0.0, jnp.concatenate(pieces, axis=0), -jnp.inf)
            cidx = cidx_ref[...]
            ca = ca_ref[...]
            a_iota = lax.broadcasted_iota(jnp.int32, (P_TOPK, LANES), 0).astype(F32)
            cut = jnp.zeros((P_TOPK, LANES), F32)
            mass = jnp.zeros((1, LANES), F32)
            for r in range(P_TOPK):
                m, hit = _take_max(cand, cidx)
                cand = jnp.where(hit, -jnp.inf, cand)
                a_pick = jnp.max(jnp.where(hit, ca, -1.0), axis=0, keepdims=True)
                cut = cut + jnp.where(a_iota == a_pick, 1.0, 0.0)
                m0 = m if r == 0 else m0
                mass = mass + jnp.exp(m - m0)
            cutd = jnp.zeros((N_KEYS, LANES), F32)
            for a in range(P_TOPK):
                cutd = jnp.where(rank0 == float(a), cut[a:a + 1, :], cutd)
            p0 = jnp.where(rank0 < float(P_TOPK), jnp.exp(s0 - ts0[0:1, :]), 0.0)
            p1 = jnp.where(rank1 < float(P_TOPK), jnp.exp(s1 - ts1[0:1, :]), 0.0) / mass
            r1_ref[h, :, cols] = rank1.astype(BF16)
            p1_ref[h, :, cols] = p1.astype(BF16)
            cut_ref[h, :, cols] = cutd
            p0_ref[h, :, cols] = p0
            return 0

        lax.fori_loop(0, t // LANES, per_tile, 0)
        return 0

    lax.fori_loop(0, P_HEADS, per_head, 0)


def _post(attn, ch, x2d, woa, wob, g1, b1, wq_b, sk_b, alpha, tb):
    n, d = x2d.shape
    bw, aw = attn.shape[1], ch.shape[1]
    assert tb % LANES == 0
    wide = lambda v: jnp.broadcast_to(jnp.asarray(v, F32).reshape(N_CAND, 1), (N_CAND, LANES))
    ca = wide(_CAND_A)
    cidx = wide([a * P_TOPK + b for a, b in zip(_CAND_A, _CAND_B)])
    cval = wide(_CAND_VALID)
    full = lambda a: pl.BlockSpec(a.shape, lambda i: (0,) * a.ndim)
    tok = lambda w: pl.BlockSpec((tb, w), lambda i: (i, 0))
    gate_spec = pl.BlockSpec((P_HEADS, N_KEYS, tb), lambda i: (0, 0, i))
    return pl.pallas_call(
        functools.partial(_post_kernel, alpha=alpha),
        grid=(n // tb,),
        in_specs=[tok(bw), tok(aw), tok(d), full(woa), full(wob), full(g1), full(b1), full(wq_b), full(sk_b),
                  full(ca), full(cidx), full(cval)],
        out_specs=[pl.BlockSpec((d, tb), lambda i: (0, i)), gate_spec, gate_spec, gate_spec, gate_spec],
        out_shape=[jax.ShapeDtypeStruct((d, n), F32),
                   jax.ShapeDtypeStruct((P_HEADS, N_KEYS, n), BF16), jax.ShapeDtypeStruct((P_HEADS, N_KEYS, n), BF16),
                   jax.ShapeDtypeStruct((P_HEADS, N_KEYS, n), F32), jax.ShapeDtypeStruct((P_HEADS, N_KEYS, n), F32)],
        scratch_shapes=[pltpu.VMEM((2, N_KEYS, tb), F32)],
        compiler_params=_cparams(("arbitrary",)),
        name="post",
    )(attn, ch, x2d, woa, wob, g1, b1, wq_b, sk_b, ca, cidx, cval)


EXPERT_TILE = 1024


def _peer_kernel(x1t_ref, r1_ref, p1_ref, cut_ref, p0_ref, u_ref, vt_ref, vtp_ref, p_ref, g2_ref, b2_ref,
                 gwt_ref, gb_ref, pwt_ref, y_ref, acc_scr, xb_scr, ga_scr, gb_scr, *, alpha):
    e = pl.program_id(1)
    last = pl.num_programs(1) - 1
    half = EXPERT_TILE // 2

    @pl.when(e == 0)
    def _():
        acc_scr[...] = jnp.zeros(acc_scr.shape, F32)
        xb_scr[...] = x1t_ref[...].astype(BF16)
        gb_scr[...] = jnp.zeros(gb_scr.shape, BF16)

    def gate(ht, part, out_scr):
        for k in range(half // N_KEYS):
            ib = part * (half // N_KEYS) + k
            w = None
            for h in range(P_HEADS):
                cut_row = cut_ref[h, ib:ib + 1, :].astype(BF16)
                p0_row = p0_ref[h, ib:ib + 1, :].astype(BF16)
                term = jnp.where(r1_ref[h] < cut_row, p1_ref[h], jnp.zeros((), BF16)) * p0_row
                w = term if w is None else w + term
            rows = slice(k * N_KEYS, (k + 1) * N_KEYS)
            out_scr[rows, :] = (_gelu(ht[rows, :]) * w.astype(F32)).astype(BF16)

    @pl.when(e < last)
    def _():
        ht_a = _nn(u_ref[0:half, :], xb_scr[...])
        gate(ht_a, 0, ga_scr)
        ht_b = _nn(u_ref[half:EXPERT_TILE, :], xb_scr[...])
        acc_scr[...] += _nn(vt_ref[...], ga_scr[...]) + _nn(vtp_ref[...], gb_scr[...])
        gate(ht_b, 1, gb_scr)

    @pl.when(e == last)
    def _():
        z = alpha * x1t_ref[...] + acc_scr[...] + _nn(vtp_ref[...], gb_scr[...])
        mu = jnp.mean(z, axis=0, keepdims=True)
        cen = z - mu
        var = jnp.mean(cen * cen, axis=0, keepdims=True)
        x2 = cen * lax.rsqrt(var + LN_EPS) * g2_ref[...] + b2_ref[...]
        gate_v = jax.nn.sigmoid(_nn(gwt_ref[...], x2.astype(BF16)) + gb_ref[...])
        ple = _nt(pwt_ref[...], p_ref[...].astype(BF16))
        y_ref[...] = (x2 + gate_v * ple).T


def _peer(x1t, r1, p1, cut, p0, u_b, vt_b, p2d, g2c, b2c, gwt, gbc, pwt, alpha, tb):
    d, n = x1t.shape
    n_exp = u_b.shape[0]
    kpt = EXPERT_TILE // N_KEYS
    half = EXPERT_TILE // 2
    n_tiles = n_exp // EXPERT_TILE
    tile = lambda e: jnp.minimum(e, n_tiles - 1)
    tcol = lambda rows: pl.BlockSpec((rows, tb), lambda t, e: (0, t))
    full = lambda a: pl.BlockSpec(a.shape, lambda t, e: (0,) * a.ndim)
    return pl.pallas_call(
        functools.partial(_peer_kernel, alpha=alpha),
        grid=(n // tb, n_tiles + 1),
        in_specs=[tcol(d),
                  pl.BlockSpec((P_HEADS, N_KEYS, tb), lambda t, e: (0, 0, t)),
                  pl.BlockSpec((P_HEADS, N_KEYS, tb), lambda t, e: (0, 0, t)),
                  pl.BlockSpec((P_HEADS, kpt, tb), lambda t, e: (0, tile(e), t)),
                  pl.BlockSpec((P_HEADS, kpt, tb), lambda t, e: (0, tile(e), t)),
                  pl.BlockSpec((EXPERT_TILE, d), lambda t, e: (tile(e), 0)),
                  pl.BlockSpec((d, half), lambda t, e: (0, 2 * tile(e))),
                  pl.BlockSpec((d, half), lambda t, e: (0, jnp.maximum(2 * e - 1, 0))),
                  pl.BlockSpec((tb, p2d.shape[1]), lambda t, e: (t, 0)),
                  full(g2c), full(b2c), full(gwt), full(gbc), full(pwt)],
        out_specs=pl.BlockSpec((tb, d), lambda t, e: (t, 0)),
        out_shape=jax.ShapeDtypeStruct((n, d), F32),
        scratch_shapes=[pltpu.VMEM((d, tb), F32), pltpu.VMEM((d, tb), BF16),
                        pltpu.VMEM((half, tb), BF16), pltpu.VMEM((half, tb), BF16)],
        compiler_params=_cparams(("arbitrary", "arbitrary")),
        name="peer",
    )(x1t, r1, p1, cut, p0, u_b, vt_b, vt_b, p2d, g2c, b2c, gwt, gbc, pwt)


def _finish(x2d, attn, ch, p2d, prm, alpha, tb):
    x1t, r1, p1, cut, p0 = _post(attn, ch, x2d, prm["woa"], prm["wob"], prm["g1"], prm["b1"], prm["wq_b"],
                                 prm["sk_b"], alpha, tb)
    return _peer(x1t, r1, p1, cut, p0, prm["u_b"], prm["vt_b"], p2d, prm["g2c"], prm["b2c"], prm["gwt"],
                 prm["gbc"], prm["pwt"], alpha, tb)


def kernel(x_prompt, x_sample, cache_k, cache_v, page_table, p_prompt, p_sample, w_in, vnorm_g, vnorm_b, w_spatial, b_spatial, w_out, ln1_g, ln1_b, peer_wq, peer_subkeys, peer_u, peer_v, ln2_g, ln2_b, ple_w, ple_gate_w, ple_gate_b):
    batch, seq, d = x_prompt.shape
    n_seq, t_new, _ = x_sample.shape
    depth = w_in.shape[0]
    n_pages = page_table.shape[1]
    past = n_pages * PAGE
    h_a = vnorm_g.shape[1]
    aw = h_a * HEAD_DIM
    bw = (w_in.shape[2] - 2 * aw) // 3
    h_b = bw // HEAD_DIM
    assert t_new == 1 and seq % BLOCK == 0 and past % BLOCK == 0 and n_pages % PAGES_PER_STEP == 0
    assert bw % LANES == 0 and aw % LANES == 0
    assert cache_k.shape[2:] == (PAGE, h_b, HEAD_DIM)
    alpha = (2.0 * depth) ** 0.25

    tt = 512 if seq % 512 == 0 else BLOCK
    tb_p = 512 if (batch * seq) % 512 == 0 else BLOCK

    cos_p, sin_p = _rope_tables(jnp.arange(seq))
    cos_s, sin_s = _rope_tables(jnp.full((n_seq,), past, jnp.int32))
    avg = _head_avg_matrix(aw)
    tril = jnp.tril(jnp.ones((CHUNK, CHUNK), F32))
    pt_flat = page_table.reshape(-1).astype(jnp.int32)
    cache_kt = jnp.transpose(cache_k, (0, 1, 3, 4, 2))
    cache_vt = jnp.transpose(cache_v, (0, 1, 3, 4, 2))

    xp = x_prompt.reshape(batch * seq, d)
    xs = x_sample.reshape(n_seq, d)
    kp_l, vp_l, ks_l, vs_l, cs_l = [], [], [], [], []
    for i in range(depth):
        w_in_b = w_in[i].astype(BF16)
        g_row = vnorm_g[i].reshape(1, aw)
        b_row = vnorm_b[i].reshape(1, aw)
        wsp_b = (w_spatial[i] * tril).astype(BF16)
        bsp_full = jnp.repeat(b_spatial[i].T, HEAD_DIM, axis=1)
        w00_row = jnp.repeat(w_spatial[i][:, 0, 0], HEAD_DIM).reshape(1, aw)
        b0_row = jnp.repeat(b_spatial[i][:, 0], HEAD_DIM).reshape(1, aw)
        prm = dict(
            woa=w_out[i][:bw].astype(BF16), wob=w_out[i][bw:].astype(BF16),
            g1=ln1_g[i].reshape(1, d), b1=ln1_b[i].reshape(1, d),
            wq_b=peer_wq[i].T.astype(BF16),
            sk_b=peer_subkeys[i].reshape(P_HEADS * 2, N_KEYS, KEY_DIM).astype(BF16),
            u_b=peer_u[i].astype(BF16), vt_b=peer_v[i].T.astype(BF16),
            g2c=ln2_g[i].reshape(d, 1), b2c=ln2_b[i].reshape(d, 1),
            gwt=ple_gate_w[i].T.astype(BF16), gbc=ple_gate_b[i].reshape(d, 1), pwt=ple_w[i].T.astype(BF16))

        qp, kp, vp, vtp, chp = _mix_prompt(xp, w_in_b, cos_p, sin_p, avg, g_row, b_row, wsp_b, bsp_full,
                                           batch, seq, bw, aw, tt)
        attn_p = _moba_prompt(qp, kp, vtp, batch, seq, bw)
        qs, kn, vn, cvs, chs = _mix_sample(xs, w_in_b, cos_s, sin_s, avg, g_row, b_row, w00_row, b0_row, bw, aw)
        sel = _sample_block_choice(cache_kt, i, pt_flat, qs, n_seq, n_pages, h_b)
        attn_s = _sample_attention(cache_kt, cache_vt, i, pt_flat, sel.reshape(-1), qs, kn, vn, n_seq, n_pages, h_b)

        xp = _finish(xp, attn_p, chp, p_prompt[i].reshape(batch * seq, -1), prm, alpha, tb_p)
        xs = _finish(xs, attn_s, chs, p_sample[i].reshape(n_seq, -1), prm, alpha, n_seq)
        kp_l.append(kp.reshape(batch, seq, h_b, HEAD_DIM))
        vp_l.append(vp.reshape(batch, seq, h_b, HEAD_DIM))
        ks_l.append(kn.reshape(n_seq, 1, h_b, HEAD_DIM))
        vs_l.append(vn.reshape(n_seq, 1, h_b, HEAD_DIM))
        cs_l.append(cvs.reshape(n_seq, 1, h_a, HEAD_DIM))
    return (xp.reshape(batch, seq, d), xs.reshape(n_seq, 1, d), jnp.stack(kp_l), jnp.stack(vp_l),
            jnp.stack(ks_l), jnp.stack(vs_l), jnp.stack(cs_l))
```

```python
import functools
import math

import jax
import jax.numpy as jnp
from jax import lax
from jax.experimental import pallas as pl
from jax.experimental.pallas import tpu as pltpu

HEAD_DIM = 64
BLOCK = 256
PAGE = 128
CHUNK = 128
TOPK_BLOCKS = 3
N_KEYS = 128
P_HEADS = 8
KEY_DIM = 128
P_TOPK = 16
LN_EPS = 1e-5
ROPE_THETA = 10000.0
LANES = 128
NEG = -1e30

VMEM_LIMIT = 56 * 1024 * 1024

F32 = jnp.float32
BF16 = jnp.bfloat16


def _nt(a, b):
    return lax.dot_general(a, b, (((1,), (1,)), ((), ())), preferred_element_type=F32)


def _nn(a, b):
    return jnp.dot(a, b, preferred_element_type=F32)


def _split(a):
    hi = a.astype(BF16)
    lo = (a - hi.astype(F32)).astype(BF16)
    return hi, lo


def _gelu(x):
    return 0.5 * x * (1.0 + lax.erf(x * (1.0 / math.sqrt(2.0))))


def _cparams(sem):
    return pltpu.CompilerParams(dimension_semantics=sem, vmem_limit_bytes=VMEM_LIMIT)


def _rope_cols(a, cos, sin_signed):
    outs = []
    lane = lax.broadcasted_iota(jnp.int32, (1, LANES), 1)
    first_half = (lane % HEAD_DIM) < (HEAD_DIM // 2)
    for c in range(a.shape[1] // LANES):
        ac = a[:, c * LANES:(c + 1) * LANES]
        partner = jnp.where(first_half, pltpu.roll(ac, LANES - HEAD_DIM // 2, 1), pltpu.roll(ac, HEAD_DIM // 2, 1))
        outs.append(ac * cos + partner * sin_signed)
    return jnp.concatenate(outs, axis=1)


def _mix_core(x_ref, w_ref, cos_ref, sin_ref, avg_ref, g_ref, b_ref, bw, aw):
    z = _nn(x_ref[...].astype(BF16), w_ref[...])
    cos = cos_ref[...]
    sin = sin_ref[...]
    q = _rope_cols(z[:, 0:bw], cos, sin)
    k = _rope_cols(z[:, bw:2 * bw], cos, sin)
    v = z[:, 2 * bw:3 * bw]
    u = _gelu(z[:, 3 * bw:3 * bw + aw])
    gv = _gelu(z[:, 3 * bw + aw:3 * bw + 2 * aw])
    avg = avg_ref[...]
    gh, gl = _split(gv)
    mu = _nn(gh, avg) + _nn(gl, avg)
    cen = gv - mu
    sh, sl = _split(cen * cen)
    var = _nn(sh, avg) + _nn(sl, avg)
    cvn = cen * lax.rsqrt(var + LN_EPS) * g_ref[...] + b_ref[...]
    return q, k, v, u, cvn


def _mix_prompt_kernel(x_ref, w_ref, cos_ref, sin_ref, avg_ref, g_ref, b_ref, wsp_ref, bsp_ref,
                       q_ref, k_ref, v_ref, vt_ref, ch_ref, *, bw, aw):
    q, k, v, u, cvn = _mix_core(x_ref, w_ref, cos_ref, sin_ref, avg_ref, g_ref, b_ref, bw, aw)
    q_ref[...] = q
    k_ref[...] = k
    v_ref[...] = v
    vt_ref[0] = v.T.astype(BF16)
    lane = lax.broadcasted_iota(jnp.int32, (1, LANES), 1)
    low = lane < HEAD_DIM
    cvb = cvn.astype(BF16)
    for c in range(cvn.shape[0] // CHUNK):
        rows = slice(c * CHUNK, (c + 1) * CHUNK)
        for p in range(aw // LANES):
            cols = slice(p * LANES, (p + 1) * LANES)
            cp = cvb[rows, cols]
            mixed = jnp.where(low, _nn(wsp_ref[2 * p], cp), _nn(wsp_ref[2 * p + 1], cp)) + bsp_ref[:, cols]
            ch_ref[rows, cols] = (u[rows, cols] * mixed).astype(BF16)


def _mix_sample_kernel(x_ref, w_ref, cos_ref, sin_ref, avg_ref, g_ref, b_ref, w00_ref, b0_ref,
                       q_ref, k_ref, v_ref, cv_ref, ch_ref, *, bw, aw):
    q, k, v, u, cvn = _mix_core(x_ref, w_ref, cos_ref, sin_ref, avg_ref, g_ref, b_ref, bw, aw)
    q_ref[...] = q
    k_ref[...] = k
    v_ref[...] = v
    cv_ref[...] = cvn
    ch_ref[...] = (u * (cvn * w00_ref[...] + b0_ref[...])).astype(BF16)


def _rope_tables(pos):
    half = HEAD_DIM // 2
    inv_freq = ROPE_THETA ** (-jnp.arange(half, dtype=F32) / half)
    ang = pos.astype(F32)[:, None] * inv_freq[None, :]
    cos, sin = jnp.cos(ang), jnp.sin(ang)
    cos_t = jnp.concatenate([cos, cos, cos, cos], axis=1)
    sin_t = jnp.concatenate([-sin, sin, -sin, sin], axis=1)
    return cos_t, sin_t


def _head_avg_matrix(width):
    h = jnp.arange(width) // HEAD_DIM
    return jnp.where(h[:, None] == h[None, :], 1.0 / HEAD_DIM, 0.0).astype(BF16)


def _mix_prompt(x2d, w_in_b, cos_t, sin_t, avg, g_row, b_row, wsp_b, bsp_full, batch, seq, bw, aw, tt):
    n, d = x2d.shape
    tiles_per_seq = seq // tt
    grid = (n // tt,)
    full = lambda shape: pl.BlockSpec(shape, lambda i: (0,) * len(shape))
    tok = lambda w: pl.BlockSpec((tt, w), lambda i: (i, 0))
    return pl.pallas_call(
        functools.partial(_mix_prompt_kernel, bw=bw, aw=aw),
        grid=grid,
        in_specs=[tok(d), full(w_in_b.shape),
                  pl.BlockSpec((tt, LANES), lambda i: (i % tiles_per_seq, 0)),
                  pl.BlockSpec((tt, LANES), lambda i: (i % tiles_per_seq, 0)),
                  full(avg.shape), full(g_row.shape), full(b_row.shape), full(wsp_b.shape), full(bsp_full.shape)],
        out_specs=[tok(bw), tok(bw), tok(bw),
                   pl.BlockSpec((1, bw, tt), lambda i: (i // tiles_per_seq, 0, i % tiles_per_seq)),
                   tok(aw)],
        out_shape=[jax.ShapeDtypeStruct((n, bw), F32), jax.ShapeDtypeStruct((n, bw), F32),
                   jax.ShapeDtypeStruct((n, bw), F32), jax.ShapeDtypeStruct((batch, bw, seq), BF16),
                   jax.ShapeDtypeStruct((n, aw), BF16)],
        compiler_params=_cparams(("arbitrary",)),
        name="mix_prompt",
    )(x2d, w_in_b, cos_t, sin_t, avg, g_row, b_row, wsp_b, bsp_full)


def _mix_sample(x2d, w_in_b, cos_t, sin_t, avg, g_row, b_row, w00_row, b0_row, bw, aw):
    n, d = x2d.shape
    full = lambda shape: pl.BlockSpec(shape, lambda i: (0,) * len(shape))
    ins = (x2d, w_in_b, cos_t, sin_t, avg, g_row, b_row, w00_row, b0_row)
    return pl.pallas_call(
        functools.partial(_mix_sample_kernel, bw=bw, aw=aw),
        grid=(1,),
        in_specs=[full(a.shape) for a in ins],
        out_specs=[full((n, bw))] * 3 + [full((n, aw))] * 2,
        out_shape=[jax.ShapeDtypeStruct((n, bw), F32)] * 3
                  + [jax.ShapeDtypeStruct((n, aw), F32), jax.ShapeDtypeStruct((n, aw), BF16)],
        compiler_params=_cparams(("arbitrary",)),
        name="mix_sample",
    )(*ins)


KV_UNROLL = 4
MASK_BIAS = -30000.0


def _top_rows(sc, rows, k):
    sel = jnp.zeros(sc.shape, F32)
    big = float(sc.shape[0])
    for _ in range(k):
        m = jnp.max(sc, axis=0, keepdims=True)
        first = jnp.min(jnp.where(sc == m, rows, big), axis=0, keepdims=True)
        hit = rows == first
        sel = jnp.where(hit & (m > -jnp.inf), 1.0, sel)
        sc = jnp.where(hit, -jnp.inf, sc)
    return sel


def _moba_prompt_kernel(q_ref, k_ref, vt_ref, o_ref, kmean_scr, *, nblocks):
    i = pl.program_id(2)
    tag_rows = HEAD_DIM

    @pl.when(i == 0)
    def _():
        kmean_scr[...] = jnp.zeros(kmean_scr.shape, F32)
        for nb in range(nblocks):
            kmean_scr[nb:nb + 1, :] = jnp.sum(k_ref[0, nb * BLOCK:(nb + 1) * BLOCK, :], axis=0, keepdims=True) * (1.0 / BLOCK)

    lane = lax.broadcasted_iota(jnp.int32, (1, LANES), 1)
    q2 = q_ref[...]
    kmh, kml = _split(kmean_scr[...])
    blk = lax.broadcasted_iota(jnp.int32, (tag_rows, BLOCK), 0).astype(F32)
    zeros = jnp.zeros((tag_rows, BLOCK), F32)
    mine, tag_base, qa = [], [], []
    for h in range(2):
        mine.append((lane // HEAD_DIM) == h)
        tag_base.append(HEAD_DIM * (1 - h))
        qh, ql = _split(jnp.where(mine[h], q2, 0.0))
        sc = _nt(kmh, qh) + _nt(kmh, ql) + _nt(kml, qh)
        sc = jnp.where(blk < i.astype(F32), sc, -jnp.inf)
        bias_t = jnp.where(_top_rows(sc, blk, TOPK_BLOCKS) > 0.0, 0.0, MASK_BIAS)
        bias = jnp.concatenate([zeros, bias_t] if h == 0 else [bias_t, zeros], axis=0).T
        qa.append(jnp.where(mine[h], q2 * (1.0 / math.sqrt(HEAD_DIM)), bias).astype(BF16))

    key_row = lax.broadcasted_iota(jnp.int32, (BLOCK, BLOCK), 0)
    qry_col = lax.broadcasted_iota(jnp.int32, (BLOCK, BLOCK), 1)

    def step(n0, nblk, carry, own):
        start = pl.multiple_of(n0 * BLOCK, BLOCK)
        kbs = [k_ref[0, pl.ds(start + u * BLOCK, BLOCK), :].astype(BF16) for u in range(nblk)]
        out = []
        for h in range(2):
            m, l, acc = carry[h]
            ss = []
            for u in range(nblk):
                if own:
                    s = jnp.where(key_row <= qry_col, _nt(jnp.where(mine[h], kbs[u], jnp.zeros((), BF16)), qa[h]), NEG)
                else:
                    tag = (lane == (tag_base[h] + n0 + u)).astype(BF16)
                    s = _nt(jnp.where(mine[h], kbs[u], tag), qa[h])
                ss.append(s)
            mc = ss[0]
            for s in ss[1:]:
                mc = jnp.maximum(mc, s)
            m_new = jnp.maximum(m, jnp.max(mc, axis=0, keepdims=True))
            alpha = jnp.exp(m - m_new)
            ps = [jnp.exp(s - m_new) for s in ss]
            psum = ps[0]
            for p in ps[1:]:
                psum = psum + p
            pcat = jnp.concatenate([p.astype(BF16) for p in ps], axis=0) if nblk > 1 else ps[0].astype(BF16)
            vtb = vt_ref[0, h * HEAD_DIM:(h + 1) * HEAD_DIM, pl.ds(start, nblk * BLOCK)]
            out.append((m_new, alpha * l + jnp.sum(psum, axis=0, keepdims=True), alpha * acc + _nn(vtb, pcat)))
        return tuple(out)

    init = tuple((jnp.full((1, BLOCK), NEG, F32), jnp.zeros((1, BLOCK), F32), jnp.zeros((HEAD_DIM, BLOCK), F32))
                 for _ in range(2))
    n_full = lax.div(i, KV_UNROLL)
    carry = lax.fori_loop(0, n_full, lambda g, c: step(g * KV_UNROLL, KV_UNROLL, c, False), init)
    carry = lax.fori_loop(n_full * KV_UNROLL, i, lambda j, c: step(j, 1, c, False), carry)
    carry = step(i, 1, carry, True)
    o_ref[...] = jnp.concatenate([acc / l for _, l, acc in carry], axis=0).T.astype(BF16)


def _moba_prompt(q, k, vt, batch, seq, bw):
    n = q.shape[0]
    nblocks = seq // BLOCK
    assert nblocks <= HEAD_DIM
    pairs = bw // LANES
    k3 = k.reshape(batch, seq, bw)
    return pl.pallas_call(
        functools.partial(_moba_prompt_kernel, nblocks=nblocks),
        grid=(batch, pairs, nblocks),
        in_specs=[pl.BlockSpec((BLOCK, LANES), lambda b, p, i: (b * nblocks + i, p)),
                  pl.BlockSpec((1, seq, LANES), lambda b, p, i: (b, 0, p)),
                  pl.BlockSpec((1, LANES, seq), lambda b, p, i: (b, p, 0))],
        out_specs=pl.BlockSpec((BLOCK, LANES), lambda b, p, i: (b * nblocks + i, p)),
        out_shape=jax.ShapeDtypeStruct((n, bw), BF16),
        scratch_shapes=[pltpu.VMEM((HEAD_DIM, LANES), F32)],
        compiler_params=_cparams(("arbitrary", "arbitrary", "arbitrary")),
        name="moba_prompt",
    )(q, k3, vt)


PAGES_PER_STEP = 16
PAGES_PER_BLOCK = BLOCK // PAGE


def _pages_kernel(pt_ref, q_ref, *rest, nblocks):
    page_refs = rest[:PAGES_PER_STEP]
    sel_ref = rest[PAGES_PER_STEP]
    part_scr = rest[PAGES_PER_STEP + 1]
    g = pl.program_id(1)
    bps = PAGES_PER_STEP // PAGES_PER_BLOCK
    qcol = q_ref[0]
    for b in range(bps):
        acc = page_refs[b * PAGES_PER_BLOCK][0, 0]
        for r in range(1, PAGES_PER_BLOCK):
            acc = acc + page_refs[b * PAGES_PER_BLOCK + r][0, 0]
        part_scr[g * bps + b] = jnp.sum(acc * qcol, axis=1)

    @pl.when(g == pl.num_programs(1) - 1)
    def _():
        sc = jnp.sum(part_scr[...], axis=-1, keepdims=True) * (1.0 / BLOCK)
        rows = lax.broadcasted_iota(jnp.int32, sc.shape, 0).astype(F32)
        picks = []
        for _ in range(TOPK_BLOCKS):
            m = jnp.max(sc, axis=0, keepdims=True)
            first = jnp.min(jnp.where(sc == m, rows, float(nblocks)), axis=0, keepdims=True)
            picks.append(first)
            sc = jnp.where(rows == first, -jnp.inf, sc)
        sel_ref[0] = jnp.concatenate(picks, axis=0).astype(jnp.int32)


def _sample_block_choice(cache_kt, layer, pt_flat, q_s, n_seq, n_pages, heads):
    nblocks = n_pages // PAGES_PER_BLOCK
    assert nblocks >= TOPK_BLOCKS
    steps = n_pages // PAGES_PER_STEP

    def page_spec(r):
        return pl.BlockSpec((1, 1, heads, HEAD_DIM, PAGE),
                            lambda s, g, pt: (layer, pt[s * n_pages + g * PAGES_PER_STEP + r], 0, 0, 0))

    grid_spec = pltpu.PrefetchScalarGridSpec(
        num_scalar_prefetch=1,
        grid=(n_seq, steps),
        in_specs=[pl.BlockSpec((1, heads, HEAD_DIM, 1), lambda s, g, pt: (s, 0, 0, 0))]
                 + [page_spec(r) for r in range(PAGES_PER_STEP)],
        out_specs=pl.BlockSpec((1, TOPK_BLOCKS, heads, 1), lambda s, g, pt: (s, 0, 0, 0)),
        scratch_shapes=[pltpu.VMEM((nblocks, heads, PAGE), F32)],
    )
    return pl.pallas_call(
        functools.partial(_pages_kernel, nblocks=nblocks),
        grid_spec=grid_spec,
        out_shape=jax.ShapeDtypeStruct((n_seq, TOPK_BLOCKS, heads, 1), jnp.int32),
        compiler_params=_cparams(("arbitrary", "arbitrary")),
        name="sample_block_choice",
    )(pt_flat, q_s.reshape(n_seq, heads, HEAD_DIM, 1), *([cache_kt] * PAGES_PER_STEP))


N_FETCH = TOPK_BLOCKS * PAGES_PER_BLOCK


def _sattn_kernel(pt_ref, sel_ref, q_ref, kn_ref, vn_ref, ck_hbm, cv_hbm, o_ref, kbuf, vbuf, sem,
                  *, layer, n_pages, heads):
    s = pl.program_id(0)
    n = pl.num_programs(0)

    def copies(seq, slot):
        out = []
        for h in range(heads):
            for r in range(N_FETCH):
                pick, pg = divmod(r, PAGES_PER_BLOCK)
                blk = sel_ref[(seq * TOPK_BLOCKS + pick) * heads + h]
                page = pt_ref[seq * n_pages + blk * PAGES_PER_BLOCK + pg]
                dst = pl.ds(r * PAGE, PAGE)
                out.append(pltpu.make_async_copy(ck_hbm.at[layer, page, h], kbuf.at[slot, h, :, dst], sem.at[0, slot]))
                out.append(pltpu.make_async_copy(cv_hbm.at[layer, page, h], vbuf.at[slot, h, :, dst], sem.at[1, slot]))
        return out

    slot = lax.rem(s, 2)

    @pl.when(s == 0)
    def _():
        for c in copies(0, 0):
            c.start()

    @pl.when(s + 1 < n)
    def _():
        for c in copies(s + 1, 1 - slot):
            c.start()

    for c in copies(s, slot):
        c.wait()

    q3 = q_ref[0]
    kn3 = kn_ref[0]
    vn3 = vn_ref[0]
    outs = []
    for h in range(heads):
        qh = q3[h:h + 1, :] * (1.0 / math.sqrt(HEAD_DIM))
        qb = qh.astype(BF16)
        sc = _nn(jnp.broadcast_to(qb, (8, HEAD_DIM)), kbuf[slot, h].astype(BF16))[0:1, :]
        s_own = jnp.sum(qb.astype(F32) * kn3[h:h + 1, :].astype(BF16).astype(F32), axis=1, keepdims=True)
        m = jnp.maximum(jnp.max(sc, axis=1, keepdims=True), s_own)
        p = jnp.exp(sc - m)
        p_own = jnp.exp(s_own - m)
        l = jnp.sum(p, axis=1, keepdims=True) + p_own
        pv = _nt(jnp.broadcast_to(p, (8, N_FETCH * PAGE)).astype(BF16), vbuf[slot, h].astype(BF16))[0:1, :]
        outs.append((pv + p_own * vn3[h:h + 1, :].astype(BF16).astype(F32)) / l)
    o_ref[0] = jnp.concatenate(outs, axis=0).astype(BF16)


def _sample_attention(cache_k, cache_v, layer, pt_flat, sel_flat, q_s, k_s, v_s, n_seq, n_pages, heads):
    row = pl.BlockSpec((1, heads, HEAD_DIM), lambda s, pt, sel: (s, 0, 0))
    anyspec = pl.BlockSpec(memory_space=pl.ANY)
    grid_spec = pltpu.PrefetchScalarGridSpec(
        num_scalar_prefetch=2,
        grid=(n_seq,),
        in_specs=[row, row, row, anyspec, anyspec],
        out_specs=row,
        scratch_shapes=[pltpu.VMEM((2, heads, HEAD_DIM, N_FETCH * PAGE), F32),
                        pltpu.VMEM((2, heads, HEAD_DIM, N_FETCH * PAGE), F32),
                        pltpu.SemaphoreType.DMA((2, 2))],
    )
    r3 = lambda a: a.reshape(n_seq, heads, HEAD_DIM)
    return pl.pallas_call(
        functools.partial(_sattn_kernel, layer=layer, n_pages=n_pages, heads=heads),
        grid_spec=grid_spec,
        out_shape=jax.ShapeDtypeStruct((n_seq, heads, HEAD_DIM), BF16),
        compiler_params=_cparams(("arbitrary",)),
        name="sample_attention",
    )(pt_flat, sel_flat, r3(q_s), r3(k_s), r3(v_s), cache_k, cache_v).reshape(n_seq, heads * HEAD_DIM)


def _cand_layout():
    a_of, b_of, valid = [], [], []
    def add(a, b):
        a_of.append(a)
        b_of.append(b)
        valid.append((a + 1) * (b + 1) <= P_TOPK)
    for b in range(16):
        add(0, b)
    for a in range(1, 8):
        for b in range(8):
            add(a, b)
    for a in range(8, 16):
        add(a, 0)
    return a_of, b_of, valid


_CAND_A, _CAND_B, _CAND_VALID = _cand_layout()
N_CAND = len(_CAND_A)
_BIG_IDX = float(1 << 20)


def _take_max(s, idx, exact):
    m = jnp.max(s, axis=0, keepdims=True)
    if not exact:
        return m, s == m
    first = jnp.min(jnp.where(s == m, idx, _BIG_IDX), axis=0, keepdims=True)
    return m, idx == first


def _top16_ranks(s, idx, exact):
    rank = jnp.full(s.shape, float(P_TOPK), F32)
    ts = []
    for r in range(P_TOPK):
        m, hit = _take_max(s, idx, exact)
        rank = jnp.where(hit, float(r), rank)
        s = jnp.where(hit, -jnp.inf, s)
        ts.append(m)
    return rank, jnp.concatenate(ts, axis=0)


_RANK_SUM = float((N_KEYS - P_TOPK) * P_TOPK + P_TOPK * (P_TOPK - 1) // 2)


def _post_kernel(attn_ref, ch_ref, x_ref, woa_ref, wob_ref, g1_ref, b1_ref, wq_ref, sk_ref,
                 ca_ref, cidx_ref, cval_ref,
                 x1t_ref, r1_ref, p1_ref, cut_ref, p0_ref, s_scr, *, alpha):
    z = _nn(attn_ref[...], woa_ref[...]) + _nn(ch_ref[...], wob_ref[...]) + alpha * x_ref[...]
    mu = jnp.mean(z, axis=1, keepdims=True)
    cen = z - mu
    var = jnp.mean(cen * cen, axis=1, keepdims=True)
    x1 = cen * lax.rsqrt(var + LN_EPS) * g1_ref[...] + b1_ref[...]
    x1t_ref[...] = x1.T
    xb = x1.astype(BF16)
    t = x1.shape[0]

    def route(exact):
        def per_head(h, bad):
            for c in range(2):
                rows = pl.ds(pl.multiple_of((h * 2 + c) * KEY_DIM, KEY_DIM), KEY_DIM)
                qt = _nt(wq_ref[rows, :], xb)
                s_scr[c] = _nn(sk_ref[h * 2 + c], qt.astype(BF16))

            def per_tile(lt, bad):
                cols = pl.ds(pl.multiple_of(lt * LANES, LANES), LANES)
                key_row = lax.broadcasted_iota(jnp.int32, (N_KEYS, LANES), 0).astype(F32)
                s0 = s_scr[0, :, cols]
                s1 = s_scr[1, :, cols]
                rank0, ts0 = _top16_ranks(s0, key_row, exact)
                rank1, ts1 = _top16_ranks(s1, key_row, exact)
                pieces = [ts0[0:1, :] + ts1]
                for a in range(1, 8):
                    pieces.append(ts0[a:a + 1, :] + ts1[0:8, :])
                pieces.append(ts0[8:16, :] + ts1[0:1, :])
                cand = jnp.where(cval_ref[...] > 0.0, jnp.concatenate(pieces, axis=0), -jnp.inf)
                cidx = cidx_ref[...]
                ca = ca_ref[...]
                a_iota = lax.broadcasted_iota(jnp.int32, (P_TOPK, LANES), 0).astype(F32)
                cut = jnp.zeros((P_TOPK, LANES), F32)
                mass = jnp.zeros((1, LANES), F32)
                for r in range(P_TOPK):
                    m, hit = _take_max(cand, cidx, exact)
                    cand = jnp.where(hit, -jnp.inf, cand)
                    a_pick = jnp.max(jnp.where(hit, ca, -1.0), axis=0, keepdims=True)
                    cut = cut + jnp.where(a_iota == a_pick, 1.0, 0.0)
                    m0 = m if r == 0 else m0
                    mass = mass + jnp.exp(m - m0)
                cutd = jnp.zeros((N_KEYS, LANES), F32)
                for a in range(P_TOPK):
                    cutd = jnp.where(rank0 == float(a), cut[a:a + 1, :], cutd)
                p0 = jnp.where(rank0 < float(P_TOPK), jnp.exp(s0 - ts0[0:1, :]), 0.0)
                p1 = jnp.where(rank1 < float(P_TOPK), jnp.exp(s1 - ts1[0:1, :]), 0.0) / mass
                r1_ref[h, :, cols] = rank1.astype(BF16)
                p1_ref[h, :, cols] = p1.astype(BF16)
                cut_ref[h, :, cols] = cutd
                p0_ref[h, :, cols] = p0
                if exact:
                    return bad
                for rank in (rank0, rank1):
                    bad = jnp.maximum(bad, jnp.abs(jnp.sum(rank, axis=0, keepdims=True) - _RANK_SUM))
                left = jnp.sum(jnp.where(cand == -jnp.inf, 0.0, 1.0), axis=0, keepdims=True)
                return jnp.maximum(bad, jnp.abs(left - float(sum(_CAND_VALID) - P_TOPK)))

            return lax.fori_loop(0, t // LANES, per_tile, bad)

        return lax.fori_loop(0, P_HEADS, per_head, jnp.zeros((1, LANES), F32))

    tie_alarm = route(False)

    @pl.when(jnp.max(tie_alarm) > 0.0)
    def _():
        route(True)


def _post(attn, ch, x2d, woa, wob, g1, b1, wq_b, sk_b, alpha, tb):
    n, d = x2d.shape
    bw, aw = attn.shape[1], ch.shape[1]
    assert tb % LANES == 0
    wide = lambda v: jnp.broadcast_to(jnp.asarray(v, F32).reshape(N_CAND, 1), (N_CAND, LANES))
    ca = wide(_CAND_A)
    cidx = wide([a * P_TOPK + b for a, b in zip(_CAND_A, _CAND_B)])
    cval = wide(_CAND_VALID)
    full = lambda a: pl.BlockSpec(a.shape, lambda i: (0,) * a.ndim)
    tok = lambda w: pl.BlockSpec((tb, w), lambda i: (i, 0))
    gate_spec = pl.BlockSpec((P_HEADS, N_KEYS, tb), lambda i: (0, 0, i))
    return pl.pallas_call(
        functools.partial(_post_kernel, alpha=alpha),
        grid=(n // tb,),
        in_specs=[tok(bw), tok(aw), tok(d), full(woa), full(wob), full(g1), full(b1), full(wq_b), full(sk_b),
                  full(ca), full(cidx), full(cval)],
        out_specs=[pl.BlockSpec((d, tb), lambda i: (0, i)), gate_spec, gate_spec, gate_spec, gate_spec],
        out_shape=[jax.ShapeDtypeStruct((d, n), F32),
                   jax.ShapeDtypeStruct((P_HEADS, N_KEYS, n), BF16), jax.ShapeDtypeStruct((P_HEADS, N_KEYS, n), BF16),
                   jax.ShapeDtypeStruct((P_HEADS, N_KEYS, n), F32), jax.ShapeDtypeStruct((P_HEADS, N_KEYS, n), F32)],
        scratch_shapes=[pltpu.VMEM((2, N_KEYS, tb), F32)],
        compiler_params=_cparams(("arbitrary",)),
        name="post",
    )(attn, ch, x2d, woa, wob, g1, b1, wq_b, sk_b, ca, cidx, cval)


EXPERT_TILE = 2048


def _peer_kernel(x1t_ref, r1_ref, p1_ref, cut_ref, p0_ref, u_ref, vt_ref, vtp_ref, p_ref, g2_ref, b2_ref,
                 gwt_ref, gb_ref, pwt_ref, y_ref, acc_scr, xb_scr, ga_scr, gb_scr, *, alpha):
    e = pl.program_id(1)
    last = pl.num_programs(1) - 1
    half = EXPERT_TILE // 2

    @pl.when(e == 0)
    def _():
        acc_scr[...] = jnp.zeros(acc_scr.shape, F32)
        xb_scr[...] = x1t_ref[...].astype(BF16)
        gb_scr[...] = jnp.zeros(gb_scr.shape, BF16)

    def gate(ht, part, out_scr):
        for k in range(half // N_KEYS):
            ib = part * (half // N_KEYS) + k
            w = None
            for h in range(P_HEADS):
                cut_row = cut_ref[h, ib:ib + 1, :].astype(BF16)
                p0_row = p0_ref[h, ib:ib + 1, :].astype(BF16)
                term = jnp.where(r1_ref[h] < cut_row, p1_ref[h], jnp.zeros((), BF16)) * p0_row
                w = term if w is None else w + term
            rows = slice(k * N_KEYS, (k + 1) * N_KEYS)
            out_scr[rows, :] = (_gelu(ht[rows, :]) * w.astype(F32)).astype(BF16)

    @pl.when(e < last)
    def _():
        ht_a = _nn(u_ref[0:half, :], xb_scr[...])
        gate(ht_a, 0, ga_scr)
        ht_b = _nn(u_ref[half:EXPERT_TILE, :], xb_scr[...])
        acc_scr[...] += _nn(vt_ref[0], ga_scr[...]) + _nn(vtp_ref[0], gb_scr[...])
        gate(ht_b, 1, gb_scr)

    @pl.when(e == last)
    def _():
        z = alpha * x1t_ref[...] + acc_scr[...] + _nn(vtp_ref[0], gb_scr[...])
        mu = jnp.mean(z, axis=0, keepdims=True)
        cen = z - mu
        var = jnp.mean(cen * cen, axis=0, keepdims=True)
        x2 = cen * lax.rsqrt(var + LN_EPS) * g2_ref[...] + b2_ref[...]
        gate_v = jax.nn.sigmoid(_nn(gwt_ref[...], x2.astype(BF16)) + gb_ref[...])
        ple = _nt(pwt_ref[...], p_ref[...].astype(BF16))
        y_ref[...] = (x2 + gate_v * ple).T


def _peer(x1t, r1, p1, cut, p0, u_b, vt_b, p2d, g2c, b2c, gwt, gbc, pwt, alpha, tb):
    d, n = x1t.shape
    n_exp = u_b.shape[0]
    kpt = EXPERT_TILE // N_KEYS
    half = EXPERT_TILE // 2
    n_tiles = n_exp // EXPERT_TILE
    tile = lambda e: jnp.minimum(e, n_tiles - 1)
    tcol = lambda rows: pl.BlockSpec((rows, tb), lambda t, e: (0, t))
    full = lambda a: pl.BlockSpec(a.shape, lambda t, e: (0,) * a.ndim)
    return pl.pallas_call(
        functools.partial(_peer_kernel, alpha=alpha),
        grid=(n // tb, n_tiles + 1),
        in_specs=[tcol(d),
                  pl.BlockSpec((P_HEADS, N_KEYS, tb), lambda t, e: (0, 0, t)),
                  pl.BlockSpec((P_HEADS, N_KEYS, tb), lambda t, e: (0, 0, t)),
                  pl.BlockSpec((P_HEADS, kpt, tb), lambda t, e: (0, tile(e), t)),
                  pl.BlockSpec((P_HEADS, kpt, tb), lambda t, e: (0, tile(e), t)),
                  pl.BlockSpec((EXPERT_TILE, d), lambda t, e: (tile(e), 0)),
                  pl.BlockSpec((1, d, half), lambda t, e: (2 * tile(e), 0, 0)),
                  pl.BlockSpec((1, d, half), lambda t, e: (jnp.maximum(2 * e - 1, 0), 0, 0)),
                  pl.BlockSpec((tb, p2d.shape[1]), lambda t, e: (t, 0)),
                  full(g2c), full(b2c), full(gwt), full(gbc), full(pwt)],
        out_specs=pl.BlockSpec((tb, d), lambda t, e: (t, 0)),
        out_shape=jax.ShapeDtypeStruct((n, d), F32),
        scratch_shapes=[pltpu.VMEM((d, tb), F32), pltpu.VMEM((d, tb), BF16),
                        pltpu.VMEM((half, tb), BF16), pltpu.VMEM((half, tb), BF16)],
        compiler_params=_cparams(("arbitrary", "arbitrary")),
        name="peer",
    )(x1t, r1, p1, cut, p0, u_b, vt_b, vt_b, p2d, g2c, b2c, gwt, gbc, pwt)


def _finish(x2d, attn, ch, p2d, prm, alpha, tb):
    x1t, r1, p1, cut, p0 = _post(attn, ch, x2d, prm["woa"], prm["wob"], prm["g1"], prm["b1"], prm["wq_b"],
                                 prm["sk_b"], alpha, tb)
    return _peer(x1t, r1, p1, cut, p0, prm["u_b"], prm["vt_b"], p2d, prm["g2c"], prm["b2c"], prm["gwt"],
                 prm["gbc"], prm["pwt"], alpha, tb)


def kernel(x_prompt, x_sample, cache_k, cache_v, page_table, p_prompt, p_sample, w_in, vnorm_g, vnorm_b, w_spatial, b_spatial, w_out, ln1_g, ln1_b, peer_wq, peer_subkeys, peer_u, peer_v, ln2_g, ln2_b, ple_w, ple_gate_w, ple_gate_b):
    batch, seq, d = x_prompt.shape
    n_seq, t_new, _ = x_sample.shape
    depth = w_in.shape[0]
    n_pages = page_table.shape[1]
    past = n_pages * PAGE
    h_a = vnorm_g.shape[1]
    aw = h_a * HEAD_DIM
    bw = (w_in.shape[2] - 2 * aw) // 3
    h_b = bw // HEAD_DIM
    assert t_new == 1 and seq % BLOCK == 0 and past % BLOCK == 0 and n_pages % PAGES_PER_STEP == 0
    assert bw % LANES == 0 and aw % LANES == 0
    assert cache_k.shape[2:] == (PAGE, h_b, HEAD_DIM)
    alpha = (2.0 * depth) ** 0.25

    tt = 512 if seq % 512 == 0 else BLOCK
    tb_p = 512 if (batch * seq) % 512 == 0 else BLOCK

    cos_p, sin_p = _rope_tables(jnp.arange(seq))
    cos_s, sin_s = _rope_tables(jnp.full((n_seq,), past, jnp.int32))
    avg = _head_avg_matrix(aw)
    tril = jnp.tril(jnp.ones((CHUNK, CHUNK), F32))
    pt_flat = page_table.reshape(-1).astype(jnp.int32)
    cache_kt = jnp.transpose(cache_k, (0, 1, 3, 4, 2))
    cache_vt = jnp.transpose(cache_v, (0, 1, 3, 4, 2))

    xp = x_prompt.reshape(batch * seq, d)
    xs = x_sample.reshape(n_seq, d)
    kp_l, vp_l, ks_l, vs_l, cs_l = [], [], [], [], []
    for i in range(depth):
        w_in_b = w_in[i].astype(BF16)
        g_row = vnorm_g[i].reshape(1, aw)
        b_row = vnorm_b[i].reshape(1, aw)
        wsp_b = (w_spatial[i] * tril).astype(BF16)
        bsp_full = jnp.repeat(b_spatial[i].T, HEAD_DIM, axis=1)
        w00_row = jnp.repeat(w_spatial[i][:, 0, 0], HEAD_DIM).reshape(1, aw)
        b0_row = jnp.repeat(b_spatial[i][:, 0], HEAD_DIM).reshape(1, aw)
        prm = dict(
            woa=w_out[i][:bw].astype(BF16), wob=w_out[i][bw:].astype(BF16),
            g1=ln1_g[i].reshape(1, d), b1=ln1_b[i].reshape(1, d),
            wq_b=peer_wq[i].T.astype(BF16),
            sk_b=peer_subkeys[i].reshape(P_HEADS * 2, N_KEYS, KEY_DIM).astype(BF16),
            u_b=peer_u[i].astype(BF16),
            vt_b=peer_v[i].reshape(-1, EXPERT_TILE // 2, d).transpose(0, 2, 1).astype(BF16),
            g2c=ln2_g[i].reshape(d, 1), b2c=ln2_b[i].reshape(d, 1),
            gwt=ple_gate_w[i].T.astype(BF16), gbc=ple_gate_b[i].reshape(d, 1), pwt=ple_w[i].T.astype(BF16))

        qp, kp, vp, vtp, chp = _mix_prompt(xp, w_in_b, cos_p, sin_p, avg, g_row, b_row, wsp_b, bsp_full,
                                           batch, seq, bw, aw, tt)
        attn_p = _moba_prompt(qp, kp, vtp, batch, seq, bw)
        qs, kn, vn, cvs, chs = _mix_sample(xs, w_in_b, cos_s, sin_s, avg, g_row, b_row, w00_row, b0_row, bw, aw)
        sel = _sample_block_choice(cache_kt, i, pt_flat, qs, n_seq, n_pages, h_b)
        attn_s = _sample_attention(cache_kt, cache_vt, i, pt_flat, sel.reshape(-1), qs, kn, vn, n_seq, n_pages, h_b)

        xp = _finish(xp, attn_p, chp, p_prompt[i].reshape(batch * seq, -1), prm, alpha, tb_p)
        xs = _finish(xs, attn_s, chs, p_sample[i].reshape(n_seq, -1), prm, alpha, n_seq)
        kp_l.append(kp.reshape(batch, seq, h_b, HEAD_DIM))
        vp_l.append(vp.reshape(batch, seq, h_b, HEAD_DIM))
        ks_l.append(kn.reshape(n_seq, 1, h_b, HEAD_DIM))
        vs_l.append(vn.reshape(n_seq, 1, h_b, HEAD_DIM))
        cs_l.append(cvs.reshape(n_seq, 1, h_a, HEAD_DIM))
    return (xp.reshape(batch, seq, d), xs.reshape(n_seq, 1, d), jnp.stack(kp_l), jnp.stack(vp_l),
            jnp.stack(ks_l), jnp.stack(vs_l), jnp.stack(cs_l))
```

```python
import functools
import math

import jax
import jax.numpy as jnp
from jax import lax
from jax.experimental import pallas as pl
from jax.experimental.pallas import tpu as pltpu

HEAD_DIM = 64
BLOCK = 256
PAGE = 128
CHUNK = 128
TOPK_BLOCKS = 3
N_KEYS = 128
P_HEADS = 8
KEY_DIM = 128
P_TOPK = 16
LN_EPS = 1e-5
ROPE_THETA = 10000.0
LANES = 128
NEG = -1e30

VMEM_LIMIT = 56 * 1024 * 1024

F32 = jnp.float32
BF16 = jnp.bfloat16


def _nt(a, b):
    return lax.dot_general(a, b, (((1,), (1,)), ((), ())), preferred_element_type=F32)


def _nn(a, b):
    return jnp.dot(a, b, preferred_element_type=F32)


def _split(a):
    hi = a.astype(BF16)
    lo = (a - hi.astype(F32)).astype(BF16)
    return hi, lo


def _gelu(x):
    return 0.5 * x * (1.0 + lax.erf(x * (1.0 / math.sqrt(2.0))))


def _cparams(sem):
    return pltpu.CompilerParams(dimension_semantics=sem, vmem_limit_bytes=VMEM_LIMIT)


def _rope_cols(a, cos, sin_signed):
    outs = []
    lane = lax.broadcasted_iota(jnp.int32, (1, LANES), 1)
    first_half = (lane % HEAD_DIM) < (HEAD_DIM // 2)
    for c in range(a.shape[1] // LANES):
        ac = a[:, c * LANES:(c + 1) * LANES]
        partner = jnp.where(first_half, pltpu.roll(ac, LANES - HEAD_DIM // 2, 1), pltpu.roll(ac, HEAD_DIM // 2, 1))
        outs.append(ac * cos + partner * sin_signed)
    return jnp.concatenate(outs, axis=1)


def _mix_core(x_ref, w_ref, cos_ref, sin_ref, avg_ref, g_ref, b_ref, bw, aw):
    z = _nn(x_ref[...].astype(BF16), w_ref[...])
    cos = cos_ref[...]
    sin = sin_ref[...]
    q = _rope_cols(z[:, 0:bw], cos, sin)
    k = _rope_cols(z[:, bw:2 * bw], cos, sin)
    v = z[:, 2 * bw:3 * bw]
    u = _gelu(z[:, 3 * bw:3 * bw + aw])
    gv = _gelu(z[:, 3 * bw + aw:3 * bw + 2 * aw])
    avg = avg_ref[...]
    gh, gl = _split(gv)
    mu = _nn(gh, avg) + _nn(gl, avg)
    cen = gv - mu
    sh, sl = _split(cen * cen)
    var = _nn(sh, avg) + _nn(sl, avg)
    cvn = cen * lax.rsqrt(var + LN_EPS) * g_ref[...] + b_ref[...]
    return q, k, v, u, cvn


def _mix_prompt_kernel(x_ref, w_ref, cos_ref, sin_ref, avg_ref, g_ref, b_ref, wsp_ref, bsp_ref,
                       q_ref, k_ref, v_ref, vt_ref, ch_ref, *, bw, aw):
    q, k, v, u, cvn = _mix_core(x_ref, w_ref, cos_ref, sin_ref, avg_ref, g_ref, b_ref, bw, aw)
    q_ref[...] = q
    k_ref[...] = k
    v_ref[...] = v
    vt_ref[0] = v.T.astype(BF16)
    lane = lax.broadcasted_iota(jnp.int32, (1, LANES), 1)
    low = lane < HEAD_DIM
    cvb = cvn.astype(BF16)
    for c in range(cvn.shape[0] // CHUNK):
        rows = slice(c * CHUNK, (c + 1) * CHUNK)
        for p in range(aw // LANES):
            cols = slice(p * LANES, (p + 1) * LANES)
            cp = cvb[rows, cols]
            mixed = jnp.where(low, _nn(wsp_ref[2 * p], cp), _nn(wsp_ref[2 * p + 1], cp)) + bsp_ref[:, cols]
            ch_ref[rows, cols] = (u[rows, cols] * mixed).astype(BF16)


def _mix_sample_kernel(x_ref, w_ref, cos_ref, sin_ref, avg_ref, g_ref, b_ref, w00_ref, b0_ref,
                       q_ref, k_ref, v_ref, cv_ref, ch_ref, *, bw, aw):
    q, k, v, u, cvn = _mix_core(x_ref, w_ref, cos_ref, sin_ref, avg_ref, g_ref, b_ref, bw, aw)
    q_ref[...] = q
    k_ref[...] = k
    v_ref[...] = v
    cv_ref[...] = cvn
    ch_ref[...] = (u * (cvn * w00_ref[...] + b0_ref[...])).astype(BF16)


def _rope_tables(pos):
    half = HEAD_DIM // 2
    inv_freq = ROPE_THETA ** (-jnp.arange(half, dtype=F32) / half)
    ang = pos.astype(F32)[:, None] * inv_freq[None, :]
    cos, sin = jnp.cos(ang), jnp.sin(ang)
    cos_t = jnp.concatenate([cos, cos, cos, cos], axis=1)
    sin_t = jnp.concatenate([-sin, sin, -sin, sin], axis=1)
    return cos_t, sin_t


def _head_avg_matrix(width):
    h = jnp.arange(width) // HEAD_DIM
    return jnp.where(h[:, None] == h[None, :], 1.0 / HEAD_DIM, 0.0).astype(BF16)


def _mix_prompt(x2d, w_in_b, cos_t, sin_t, avg, g_row, b_row, wsp_b, bsp_full, batch, seq, bw, aw, tt):
    n, d = x2d.shape
    tiles_per_seq = seq // tt
    grid = (n // tt,)
    full = lambda shape: pl.BlockSpec(shape, lambda i: (0,) * len(shape))
    tok = lambda w: pl.BlockSpec((tt, w), lambda i: (i, 0))
    return pl.pallas_call(
        functools.partial(_mix_prompt_kernel, bw=bw, aw=aw),
        grid=grid,
        in_specs=[tok(d), full(w_in_b.shape),
                  pl.BlockSpec((tt, LANES), lambda i: (i % tiles_per_seq, 0)),
                  pl.BlockSpec((tt, LANES), lambda i: (i % tiles_per_seq, 0)),
                  full(avg.shape), full(g_row.shape), full(b_row.shape), full(wsp_b.shape), full(bsp_full.shape)],
        out_specs=[tok(bw), tok(bw), tok(bw),
                   pl.BlockSpec((1, bw, tt), lambda i: (i // tiles_per_seq, 0, i % tiles_per_seq)),
                   tok(aw)],
        out_shape=[jax.ShapeDtypeStruct((n, bw), F32), jax.ShapeDtypeStruct((n, bw), F32),
                   jax.ShapeDtypeStruct((n, bw), F32), jax.ShapeDtypeStruct((batch, bw, seq), BF16),
                   jax.ShapeDtypeStruct((n, aw), BF16)],
        compiler_params=_cparams(("arbitrary",)),
        name="mix_prompt",
    )(x2d, w_in_b, cos_t, sin_t, avg, g_row, b_row, wsp_b, bsp_full)


def _mix_sample(x2d, w_in_b, cos_t, sin_t, avg, g_row, b_row, w00_row, b0_row, bw, aw):
    n, d = x2d.shape
    full = lambda shape: pl.BlockSpec(shape, lambda i: (0,) * len(shape))
    ins = (x2d, w_in_b, cos_t, sin_t, avg, g_row, b_row, w00_row, b0_row)
    return pl.pallas_call(
        functools.partial(_mix_sample_kernel, bw=bw, aw=aw),
        grid=(1,),
        in_specs=[full(a.shape) for a in ins],
        out_specs=[full((n, bw))] * 3 + [full((n, aw))] * 2,
        out_shape=[jax.ShapeDtypeStruct((n, bw), F32)] * 3
                  + [jax.ShapeDtypeStruct((n, aw), F32), jax.ShapeDtypeStruct((n, aw), BF16)],
        compiler_params=_cparams(("arbitrary",)),
        name="mix_sample",
    )(*ins)


KV_UNROLL = 4
MASK_BIAS = -30000.0


def _top_rows(sc, rows, k):
    sel = jnp.zeros(sc.shape, F32)
    big = float(sc.shape[0])
    for _ in range(k):
        m = jnp.max(sc, axis=0, keepdims=True)
        first = jnp.min(jnp.where(sc == m, rows, big), axis=0, keepdims=True)
        hit = rows == first
        sel = jnp.where(hit & (m > -jnp.inf), 1.0, sel)
        sc = jnp.where(hit, -jnp.inf, sc)
    return sel


def _moba_prompt_kernel(q_ref, k_ref, vt_ref, o_ref, kmean_scr, *, nblocks):
    i = pl.program_id(2)
    tag_rows = HEAD_DIM

    @pl.when(i == 0)
    def _():
        kmean_scr[...] = jnp.zeros(kmean_scr.shape, F32)
        for nb in range(nblocks):
            kmean_scr[nb:nb + 1, :] = jnp.sum(k_ref[0, nb * BLOCK:(nb + 1) * BLOCK, :], axis=0, keepdims=True) * (1.0 / BLOCK)

    lane = lax.broadcasted_iota(jnp.int32, (1, LANES), 1)
    q2 = q_ref[...]
    kmh, kml = _split(kmean_scr[...])
    blk = lax.broadcasted_iota(jnp.int32, (tag_rows, BLOCK), 0).astype(F32)
    zeros = jnp.zeros((tag_rows, BLOCK), F32)
    mine, tag_base, qa = [], [], []
    for h in range(2):
        mine.append((lane // HEAD_DIM) == h)
        tag_base.append(HEAD_DIM * (1 - h))
        qh, ql = _split(jnp.where(mine[h], q2, 0.0))
        sc = _nt(kmh, qh) + _nt(kmh, ql) + _nt(kml, qh)
        sc = jnp.where(blk < i.astype(F32), sc, -jnp.inf)
        bias_t = jnp.where(_top_rows(sc, blk, TOPK_BLOCKS) > 0.0, 0.0, MASK_BIAS)
        bias = jnp.concatenate([zeros, bias_t] if h == 0 else [bias_t, zeros], axis=0).T
        qa.append(jnp.where(mine[h], q2 * (1.0 / math.sqrt(HEAD_DIM)), bias).astype(BF16))

    key_row = lax.broadcasted_iota(jnp.int32, (BLOCK, BLOCK), 0)
    qry_col = lax.broadcasted_iota(jnp.int32, (BLOCK, BLOCK), 1)

    def step(n0, nblk, carry, own):
        start = pl.multiple_of(n0 * BLOCK, BLOCK)
        kbs = [k_ref[0, pl.ds(start + u * BLOCK, BLOCK), :].astype(BF16) for u in range(nblk)]
        out = []
        for h in range(2):
            m, l, acc = carry[h]
            ss = []
            for u in range(nblk):
                if own:
                    s = jnp.where(key_row <= qry_col, _nt(jnp.where(mine[h], kbs[u], jnp.zeros((), BF16)), qa[h]), NEG)
                else:
                    tag = (lane == (tag_base[h] + n0 + u)).astype(BF16)
                    s = _nt(jnp.where(mine[h], kbs[u], tag), qa[h])
                ss.append(s)
            mc = ss[0]
            for s in ss[1:]:
                mc = jnp.maximum(mc, s)
            m_new = jnp.maximum(m, jnp.max(mc, axis=0, keepdims=True))
            alpha = jnp.exp(m - m_new)
            ps = [jnp.exp(s - m_new) for s in ss]
            psum = ps[0]
            for p in ps[1:]:
                psum = psum + p
            pcat = jnp.concatenate([p.astype(BF16) for p in ps], axis=0) if nblk > 1 else ps[0].astype(BF16)
            vtb = vt_ref[0, h * HEAD_DIM:(h + 1) * HEAD_DIM, pl.ds(start, nblk * BLOCK)]
            out.append((m_new, alpha * l + jnp.sum(psum, axis=0, keepdims=True), alpha * acc + _nn(vtb, pcat)))
        return tuple(out)

    init = tuple((jnp.full((1, BLOCK), NEG, F32), jnp.zeros((1, BLOCK), F32), jnp.zeros((HEAD_DIM, BLOCK), F32))
                 for _ in range(2))
    n_full = lax.div(i, KV_UNROLL)
    carry = lax.fori_loop(0, n_full, lambda g, c: step(g * KV_UNROLL, KV_UNROLL, c, False), init)
    carry = lax.fori_loop(n_full * KV_UNROLL, i, lambda j, c: step(j, 1, c, False), carry)
    carry = step(i, 1, carry, True)
    o_ref[...] = jnp.concatenate([acc / l for _, l, acc in carry], axis=0).T.astype(BF16)


def _moba_prompt(q, k, vt, batch, seq, bw):
    n = q.shape[0]
    nblocks = seq // BLOCK
    assert nblocks <= HEAD_DIM
    pairs = bw // LANES
    k3 = k.reshape(batch, seq, bw)
    return pl.pallas_call(
        functools.partial(_moba_prompt_kernel, nblocks=nblocks),
        grid=(batch, pairs, nblocks),
        in_specs=[pl.BlockSpec((BLOCK, LANES), lambda b, p, i: (b * nblocks + i, p)),
                  pl.BlockSpec((1, seq, LANES), lambda b, p, i: (b, 0, p)),
                  pl.BlockSpec((1, LANES, seq), lambda b, p, i: (b, p, 0))],
        out_specs=pl.BlockSpec((BLOCK, LANES), lambda b, p, i: (b * nblocks + i, p)),
        out_shape=jax.ShapeDtypeStruct((n, bw), BF16),
        scratch_shapes=[pltpu.VMEM((HEAD_DIM, LANES), F32)],
        compiler_params=_cparams(("arbitrary", "arbitrary", "arbitrary")),
        name="moba_prompt",
    )(q, k3, vt)


PAGES_PER_STEP = 16
PAGES_PER_BLOCK = BLOCK // PAGE


def _pages_kernel(pt_ref, q_ref, *rest, nblocks):
    page_refs = rest[:PAGES_PER_STEP]
    sel_ref = rest[PAGES_PER_STEP]
    part_scr = rest[PAGES_PER_STEP + 1]
    g = pl.program_id(1)
    bps = PAGES_PER_STEP // PAGES_PER_BLOCK
    qcol = q_ref[0]
    for b in range(bps):
        acc = page_refs[b * PAGES_PER_BLOCK][0, 0]
        for r in range(1, PAGES_PER_BLOCK):
            acc = acc + page_refs[b * PAGES_PER_BLOCK + r][0, 0]
        part_scr[g * bps + b] = jnp.sum(acc * qcol, axis=1)

    @pl.when(g == pl.num_programs(1) - 1)
    def _():
        sc = jnp.sum(part_scr[...], axis=-1, keepdims=True) * (1.0 / BLOCK)
        rows = lax.broadcasted_iota(jnp.int32, sc.shape, 0).astype(F32)
        picks = []
        for _ in range(TOPK_BLOCKS):
            m = jnp.max(sc, axis=0, keepdims=True)
            first = jnp.min(jnp.where(sc == m, rows, float(nblocks)), axis=0, keepdims=True)
            picks.append(first)
            sc = jnp.where(rows == first, -jnp.inf, sc)
        sel_ref[0] = jnp.concatenate(picks, axis=0).astype(jnp.int32)


def _sample_block_choice(cache_kt, layer, pt_flat, q_s, n_seq, n_pages, heads):
    nblocks = n_pages // PAGES_PER_BLOCK
    assert nblocks >= TOPK_BLOCKS
    steps = n_pages // PAGES_PER_STEP

    def page_spec(r):
        return pl.BlockSpec((1, 1, heads, HEAD_DIM, PAGE),
                            lambda s, g, pt: (layer, pt[s * n_pages + g * PAGES_PER_STEP + r], 0, 0, 0))

    grid_spec = pltpu.PrefetchScalarGridSpec(
        num_scalar_prefetch=1,
        grid=(n_seq, steps),
        in_specs=[pl.BlockSpec((1, heads, HEAD_DIM, 1), lambda s, g, pt: (s, 0, 0, 0))]
                 + [page_spec(r) for r in range(PAGES_PER_STEP)],
        out_specs=pl.BlockSpec((1, TOPK_BLOCKS, heads, 1), lambda s, g, pt: (s, 0, 0, 0)),
        scratch_shapes=[pltpu.VMEM((nblocks, heads, PAGE), F32)],
    )
    return pl.pallas_call(
        functools.partial(_pages_kernel, nblocks=nblocks),
        grid_spec=grid_spec,
        out_shape=jax.ShapeDtypeStruct((n_seq, TOPK_BLOCKS, heads, 1), jnp.int32),
        compiler_params=_cparams(("arbitrary", "arbitrary")),
        name="sample_block_choice",
    )(pt_flat, q_s.reshape(n_seq, heads, HEAD_DIM, 1), *([cache_kt] * PAGES_PER_STEP))


N_FETCH = TOPK_BLOCKS * PAGES_PER_BLOCK


def _sattn_kernel(pt_ref, sel_ref, q_ref, kn_ref, vn_ref, ck_hbm, cv_hbm, o_ref, kbuf, vbuf, sem,
                  *, layer, n_pages, heads):
    s = pl.program_id(0)
    n = pl.num_programs(0)

    def copies(seq, slot):
        out = []
        for h in range(heads):
            for r in range(N_FETCH):
                pick, pg = divmod(r, PAGES_PER_BLOCK)
                blk = sel_ref[(seq * TOPK_BLOCKS + pick) * heads + h]
                page = pt_ref[seq * n_pages + blk * PAGES_PER_BLOCK + pg]
                dst = pl.ds(r * PAGE, PAGE)
                out.append(pltpu.make_async_copy(ck_hbm.at[layer, page, h], kbuf.at[slot, h, :, dst], sem.at[0, slot]))
                out.append(pltpu.make_async_copy(cv_hbm.at[layer, page, h], vbuf.at[slot, h, :, dst], sem.at[1, slot]))
        return out

    slot = lax.rem(s, 2)

    @pl.when(s == 0)
    def _():
        for c in copies(0, 0):
            c.start()

    @pl.when(s + 1 < n)
    def _():
        for c in copies(s + 1, 1 - slot):
            c.start()

    for c in copies(s, slot):
        c.wait()

    q3 = q_ref[0]
    kn3 = kn_ref[0]
    vn3 = vn_ref[0]
    outs = []
    for h in range(heads):
        qh = q3[h:h + 1, :] * (1.0 / math.sqrt(HEAD_DIM))
        qb = qh.astype(BF16)
        sc = _nn(jnp.broadcast_to(qb, (8, HEAD_DIM)), kbuf[slot, h].astype(BF16))[0:1, :]
        s_own = jnp.sum(qb.astype(F32) * kn3[h:h + 1, :].astype(BF16).astype(F32), axis=1, keepdims=True)
        m = jnp.maximum(jnp.max(sc, axis=1, keepdims=True), s_own)
        p = jnp.exp(sc - m)
        p_own = jnp.exp(s_own - m)
        l = jnp.sum(p, axis=1, keepdims=True) + p_own
        pv = _nt(jnp.broadcast_to(p, (8, N_FETCH * PAGE)).astype(BF16), vbuf[slot, h].astype(BF16))[0:1, :]
        outs.append((pv + p_own * vn3[h:h + 1, :].astype(BF16).astype(F32)) / l)
    o_ref[0] = jnp.concatenate(outs, axis=0).astype(BF16)


def _sample_attention(cache_k, cache_v, layer, pt_flat, sel_flat, q_s, k_s, v_s, n_seq, n_pages, heads):
    row = pl.BlockSpec((1, heads, HEAD_DIM), lambda s, pt, sel: (s, 0, 0))
    anyspec = pl.BlockSpec(memory_space=pl.ANY)
    grid_spec = pltpu.PrefetchScalarGridSpec(
        num_scalar_prefetch=2,
        grid=(n_seq,),
        in_specs=[row, row, row, anyspec, anyspec],
        out_specs=row,
        scratch_shapes=[pltpu.VMEM((2, heads, HEAD_DIM, N_FETCH * PAGE), F32),
                        pltpu.VMEM((2, heads, HEAD_DIM, N_FETCH * PAGE), F32),
                        pltpu.SemaphoreType.DMA((2, 2))],
    )
    r3 = lambda a: a.reshape(n_seq, heads, HEAD_DIM)
    return pl.pallas_call(
        functools.partial(_sattn_kernel, layer=layer, n_pages=n_pages, heads=heads),
        grid_spec=grid_spec,
        out_shape=jax.ShapeDtypeStruct((n_seq, heads, HEAD_DIM), BF16),
        compiler_params=_cparams(("arbitrary",)),
        name="sample_attention",
    )(pt_flat, sel_flat, r3(q_s), r3(k_s), r3(v_s), cache_k, cache_v).reshape(n_seq, heads * HEAD_DIM)


def _cand_layout():
    a_of, b_of, valid = [], [], []
    def add(a, b):
        a_of.append(a)
        b_of.append(b)
        valid.append((a + 1) * (b + 1) <= P_TOPK)
    for b in range(16):
        add(0, b)
    for a in range(1, 8):
        for b in range(8):
            add(a, b)
    for a in range(8, 16):
        add(a, 0)
    return a_of, b_of, valid


_CAND_A, _CAND_B, _CAND_VALID = _cand_layout()
N_CAND = len(_CAND_A)
_BIG_IDX = float(1 << 20)


def _take_max(s, idx, exact):
    m = jnp.max(s, axis=0, keepdims=True)
    if not exact:
        return m, s == m
    first = jnp.min(jnp.where(s == m, idx, _BIG_IDX), axis=0, keepdims=True)
    return m, idx == first


def _top16_ranks(s, idx, exact):
    rank = jnp.full(s.shape, float(P_TOPK), F32)
    ts = []
    for r in range(P_TOPK):
        m, hit = _take_max(s, idx, exact)
        rank = jnp.where(hit, float(r), rank)
        s = jnp.where(hit, -jnp.inf, s)
        ts.append(m)
    return rank, jnp.concatenate(ts, axis=0)


_RANK_SUM = float((N_KEYS - P_TOPK) * P_TOPK + P_TOPK * (P_TOPK - 1) // 2)


def _post_kernel(attn_ref, ch_ref, x_ref, woa_ref, wob_ref, g1_ref, b1_ref, wq_ref, sk_ref,
                 ca_ref, cidx_ref, cval_ref,
                 x1t_ref, r1_ref, p1_ref, cut_ref, p0_ref, s_scr, *, alpha):
    z = _nn(attn_ref[...], woa_ref[...]) + _nn(ch_ref[...], wob_ref[...]) + alpha * x_ref[...]
    mu = jnp.mean(z, axis=1, keepdims=True)
    cen = z - mu
    var = jnp.mean(cen * cen, axis=1, keepdims=True)
    x1 = cen * lax.rsqrt(var + LN_EPS) * g1_ref[...] + b1_ref[...]
    x1t_ref[...] = x1.T
    xb = x1.astype(BF16)
    t = x1.shape[0]

    def route(h, exact):
        if True:
            def per_tile(lt, bad):
                cols = pl.ds(pl.multiple_of(lt * LANES, LANES), LANES)
                key_row = lax.broadcasted_iota(jnp.int32, (N_KEYS, LANES), 0).astype(F32)
                s0 = s_scr[0, :, cols]
                s1 = s_scr[1, :, cols]
                rank0, ts0 = _top16_ranks(s0, key_row, exact)
                rank1, ts1 = _top16_ranks(s1, key_row, exact)
                pieces = [ts0[0:1, :] + ts1]
                for a in range(1, 8):
                    pieces.append(ts0[a:a + 1, :] + ts1[0:8, :])
                pieces.append(ts0[8:16, :] + ts1[0:1, :])
                cand = jnp.where(cval_ref[...] > 0.0, jnp.concatenate(pieces, axis=0), -jnp.inf)
                cidx = cidx_ref[...]
                ca = ca_ref[...]
                a_iota = lax.broadcasted_iota(jnp.int32, (P_TOPK, LANES), 0).astype(F32)
                cut = jnp.zeros((P_TOPK, LANES), F32)
                mass = jnp.zeros((1, LANES), F32)
                for r in range(P_TOPK):
                    m, hit = _take_max(cand, cidx, exact)
                    cand = jnp.where(hit, -jnp.inf, cand)
                    a_pick = jnp.max(jnp.where(hit, ca, -1.0), axis=0, keepdims=True)
                    cut = cut + jnp.where(a_iota == a_pick, 1.0, 0.0)
                    m0 = m if r == 0 else m0
                    mass = mass + jnp.exp(m - m0)
                cutd = jnp.zeros((N_KEYS, LANES), F32)
                for a in range(P_TOPK):
                    cutd = jnp.where(rank0 == float(a), cut[a:a + 1, :], cutd)
                p0 = jnp.where(rank0 < float(P_TOPK), jnp.exp(s0 - ts0[0:1, :]), 0.0)
                p1 = jnp.where(rank1 < float(P_TOPK), jnp.exp(s1 - ts1[0:1, :]), 0.0) / mass
                r1_ref[h, :, cols] = rank1.astype(BF16)
                p1_ref[h, :, cols] = p1.astype(BF16)
                cut_ref[h, :, cols] = cutd
                p0_ref[h, :, cols] = p0
                if exact:
                    return bad
                for rank in (rank0, rank1):
                    bad = jnp.maximum(bad, jnp.abs(jnp.sum(rank, axis=0, keepdims=True) - _RANK_SUM))
                left = jnp.sum(jnp.where(cand == -jnp.inf, 0.0, 1.0), axis=0, keepdims=True)
                return jnp.maximum(bad, jnp.abs(left - float(sum(_CAND_VALID) - P_TOPK)))

            return lax.fori_loop(0, t // LANES, per_tile, jnp.zeros((1, LANES), F32))

    def per_head(h, _):
        for c in range(2):
            rows = pl.ds(pl.multiple_of((h * 2 + c) * KEY_DIM, KEY_DIM), KEY_DIM)
            qt = _nt(wq_ref[rows, :], xb)
            s_scr[c] = _nn(sk_ref[h * 2 + c], qt.astype(BF16))
        tie_alarm = route(h, False)

        @pl.when(jnp.max(tie_alarm) > 0.0)
        def _():
            route(h, True)

        return 0

    lax.fori_loop(0, P_HEADS, per_head, 0)


def _post(attn, ch, x2d, woa, wob, g1, b1, wq_b, sk_b, alpha, tb):
    n, d = x2d.shape
    bw, aw = attn.shape[1], ch.shape[1]
    assert tb % LANES == 0
    wide = lambda v: jnp.broadcast_to(jnp.asarray(v, F32).reshape(N_CAND, 1), (N_CAND, LANES))
    ca = wide(_CAND_A)
    cidx = wide([a * P_TOPK + b for a, b in zip(_CAND_A, _CAND_B)])
    cval = wide(_CAND_VALID)
    full = lambda a: pl.BlockSpec(a.shape, lambda i: (0,) * a.ndim)
    tok = lambda w: pl.BlockSpec((tb, w), lambda i: (i, 0))
    gate_spec = pl.BlockSpec((P_HEADS, N_KEYS, tb), lambda i: (0, 0, i))
    return pl.pallas_call(
        functools.partial(_post_kernel, alpha=alpha),
        grid=(n // tb,),
        in_specs=[tok(bw), tok(aw), tok(d), full(woa), full(wob), full(g1), full(b1), full(wq_b), full(sk_b),
                  full(ca), full(cidx), full(cval)],
        out_specs=[pl.BlockSpec((d, tb), lambda i: (0, i)), gate_spec, gate_spec, gate_spec, gate_spec],
        out_shape=[jax.ShapeDtypeStruct((d, n), F32),
                   jax.ShapeDtypeStruct((P_HEADS, N_KEYS, n), BF16), jax.ShapeDtypeStruct((P_HEADS, N_KEYS, n), BF16),
                   jax.ShapeDtypeStruct((P_HEADS, N_KEYS, n), F32), jax.ShapeDtypeStruct((P_HEADS, N_KEYS, n), F32)],
        scratch_shapes=[pltpu.VMEM((2, N_KEYS, tb), F32)],
        compiler_params=_cparams(("arbitrary",)),
        name="post",
    )(attn, ch, x2d, woa, wob, g1, b1, wq_b, sk_b, ca, cidx, cval)


EXPERT_TILE = 1024


def _peer_kernel(x1t_ref, r1_ref, p1_ref, cut_ref, p0_ref, u_ref, vt_ref, vtp_ref, p_ref, g2_ref, b2_ref,
                 gwt_ref, gb_ref, pwt_ref, y_ref, acc_scr, xb_scr, ga_scr, gb_scr, *, alpha):
    e = pl.program_id(1)
    last = pl.num_programs(1) - 1
    half = EXPERT_TILE // 2

    @pl.when(e == 0)
    def _():
        acc_scr[...] = jnp.zeros(acc_scr.shape, F32)
        xb_scr[...] = x1t_ref[...].astype(BF16)
        gb_scr[...] = jnp.zeros(gb_scr.shape, BF16)

    def gate(ht, part, out_scr):
        for k in range(half // N_KEYS):
            ib = part * (half // N_KEYS) + k
            w = None
            for h in range(P_HEADS):
                cut_row = cut_ref[h, ib:ib + 1, :].astype(BF16)
                p0_row = p0_ref[h, ib:ib + 1, :].astype(BF16)
                term = jnp.where(r1_ref[h] < cut_row, p1_ref[h], jnp.zeros((), BF16)) * p0_row
                w = term if w is None else w + term
            rows = slice(k * N_KEYS, (k + 1) * N_KEYS)
            out_scr[rows, :] = (_gelu(ht[rows, :]) * w.astype(F32)).astype(BF16)

    @pl.when(e < last)
    def _():
        ht_a = _nn(u_ref[0:half, :], xb_scr[...])
        gate(ht_a, 0, ga_scr)
        ht_b = _nn(u_ref[half:EXPERT_TILE, :], xb_scr[...])
        acc_scr[...] += _nn(vt_ref[...], ga_scr[...]) + _nn(vtp_ref[...], gb_scr[...])
        gate(ht_b, 1, gb_scr)

    @pl.when(e == last)
    def _():
        z = alpha * x1t_ref[...] + acc_scr[...] + _nn(vtp_ref[...], gb_scr[...])
        mu = jnp.mean(z, axis=0, keepdims=True)
        cen = z - mu
        var = jnp.mean(cen * cen, axis=0, keepdims=True)
        x2 = cen * lax.rsqrt(var + LN_EPS) * g2_ref[...] + b2_ref[...]
        gate_v = jax.nn.sigmoid(_nn(gwt_ref[...], x2.astype(BF16)) + gb_ref[...])
        ple = _nt(pwt_ref[...], p_ref[...].astype(BF16))
        y_ref[...] = (x2 + gate_v * ple).T


def _peer(x1t, r1, p1, cut, p0, u_b, vt_b, p2d, g2c, b2c, gwt, gbc, pwt, alpha, tb):
    d, n = x1t.shape
    n_exp = u_b.shape[0]
    kpt = EXPERT_TILE // N_KEYS
    half = EXPERT_TILE // 2
    n_tiles = n_exp // EXPERT_TILE
    tile = lambda e: jnp.minimum(e, n_tiles - 1)
    tcol = lambda rows: pl.BlockSpec((rows, tb), lambda t, e: (0, t))
    full = lambda a: pl.BlockSpec(a.shape, lambda t, e: (0,) * a.ndim)
    return pl.pallas_call(
        functools.partial(_peer_kernel, alpha=alpha),
        grid=(n // tb, n_tiles + 1),
        in_specs=[tcol(d),
                  pl.BlockSpec((P_HEADS, N_KEYS, tb), lambda t, e: (0, 0, t)),
                  pl.BlockSpec((P_HEADS, N_KEYS, tb), lambda t, e: (0, 0, t)),
                  pl.BlockSpec((P_HEADS, kpt, tb), lambda t, e: (0, tile(e), t)),
                  pl.BlockSpec((P_HEADS, kpt, tb), lambda t, e: (0, tile(e), t)),
                  pl.BlockSpec((EXPERT_TILE, d), lambda t, e: (tile(e), 0)),
                  pl.BlockSpec((d, half), lambda t, e: (0, 2 * tile(e))),
                  pl.BlockSpec((d, half), lambda t, e: (0, jnp.maximum(2 * e - 1, 0))),
                  pl.BlockSpec((tb, p2d.shape[1]), lambda t, e: (t, 0)),
                  full(g2c), full(b2c), full(gwt), full(gbc), full(pwt)],
        out_specs=pl.BlockSpec((tb, d), lambda t, e: (t, 0)),
        out_shape=jax.ShapeDtypeStruct((n, d), F32),
        scratch_shapes=[pltpu.VMEM((d, tb), F32), pltpu.VMEM((d, tb), BF16),
                        pltpu.VMEM((half, tb), BF16), pltpu.VMEM((half, tb), BF16)],
        compiler_params=_cparams(("arbitrary", "arbitrary")),
        name="peer",
    )(x1t, r1, p1, cut, p0, u_b, vt_b, vt_b, p2d, g2c, b2c, gwt, gbc, pwt)


def _finish(x2d, attn, ch, p2d, prm, alpha, tb):
    x1t, r1, p1, cut, p0 = _post(attn, ch, x2d, prm["woa"], prm["wob"], prm["g1"], prm["b1"], prm["wq_b"],
                                 prm["sk_b"], alpha, tb)
    return _peer(x1t, r1, p1, cut, p0, prm["u_b"], prm["vt_b"], p2d, prm["g2c"], prm["b2c"], prm["gwt"],
                 prm["gbc"], prm["pwt"], alpha, tb)


def kernel(x_prompt, x_sample, cache_k, cache_v, page_table, p_prompt, p_sample, w_in, vnorm_g, vnorm_b, w_spatial, b_spatial, w_out, ln1_g, ln1_b, peer_wq, peer_subkeys, peer_u, peer_v, ln2_g, ln2_b, ple_w, ple_gate_w, ple_gate_b):
    batch, seq, d = x_prompt.shape
    n_seq, t_new, _ = x_sample.shape
    depth = w_in.shape[0]
    n_pages = page_table.shape[1]
    past = n_pages * PAGE
    h_a = vnorm_g.shape[1]
    aw = h_a * HEAD_DIM
    bw = (w_in.shape[2] - 2 * aw) // 3
    h_b = bw // HEAD_DIM
    assert t_new == 1 and seq % BLOCK == 0 and past % BLOCK == 0 and n_pages % PAGES_PER_STEP == 0
    assert bw % LANES == 0 and aw % LANES == 0
    assert cache_k.shape[2:] == (PAGE, h_b, HEAD_DIM)
    alpha = (2.0 * depth) ** 0.25

    tt = 512 if seq % 512 == 0 else BLOCK
    tb_p = 512 if (batch * seq) % 512 == 0 else BLOCK

    cos_p, sin_p = _rope_tables(jnp.arange(seq))
    cos_s, sin_s = _rope_tables(jnp.full((n_seq,), past, jnp.int32))
    avg = _head_avg_matrix(aw)
    tril = jnp.tril(jnp.ones((CHUNK, CHUNK), F32))
    pt_flat = page_table.reshape(-1).astype(jnp.int32)
    cache_kt = jnp.transpose(cache_k, (0, 1, 3, 4, 2))
    cache_vt = jnp.transpose(cache_v, (0, 1, 3, 4, 2))

    xp = x_prompt.reshape(batch * seq, d)
    xs = x_sample.reshape(n_seq, d)
    kp_l, vp_l, ks_l, vs_l, cs_l = [], [], [], [], []
    for i in range(depth):
        w_in_b = w_in[i].astype(BF16)
        g_row = vnorm_g[i].reshape(1, aw)
        b_row = vnorm_b[i].reshape(1, aw)
        wsp_b = (w_spatial[i] * tril).astype(BF16)
        bsp_full = jnp.repeat(b_spatial[i].T, HEAD_DIM, axis=1)
        w00_row = jnp.repeat(w_spatial[i][:, 0, 0], HEAD_DIM).reshape(1, aw)
        b0_row = jnp.repeat(b_spatial[i][:, 0], HEAD_DIM).reshape(1, aw)
        prm = dict(
            woa=w_out[i][:bw].astype(BF16), wob=w_out[i][bw:].astype(BF16),
            g1=ln1_g[i].reshape(1, d), b1=ln1_b[i].reshape(1, d),
            wq_b=peer_wq[i].T.astype(BF16),
            sk_b=peer_subkeys[i].reshape(P_HEADS * 2, N_KEYS, KEY_DIM).astype(BF16),
            u_b=peer_u[i].astype(BF16), vt_b=peer_v[i].T.astype(BF16),
            g2c=ln2_g[i].reshape(d, 1), b2c=ln2_b[i].reshape(d, 1),
            gwt=ple_gate_w[i].T.astype(BF16), gbc=ple_gate_b[i].reshape(d, 1), pwt=ple_w[i].T.astype(BF16))

        qp, kp, vp, vtp, chp = _mix_prompt(xp, w_in_b, cos_p, sin_p, avg, g_row, b_row, wsp_b, bsp_full,
                                           batch, seq, bw, aw, tt)
        attn_p = _moba_prompt(qp, kp, vtp, batch, seq, bw)
        qs, kn, vn, cvs, chs = _mix_sample(xs, w_in_b, cos_s, sin_s, avg, g_row, b_row, w00_row, b0_row, bw, aw)
        sel = _sample_block_choice(cache_kt, i, pt_flat, qs, n_seq, n_pages, h_b)
        attn_s = _sample_attention(cache_kt, cache_vt, i, pt_flat, sel.reshape(-1), qs, kn, vn, n_seq, n_pages, h_b)

        xp = _finish(xp, attn_p, chp, p_prompt[i].reshape(batch * seq, -1), prm, alpha, tb_p)
        xs = _finish(xs, attn_s, chs, p_sample[i].reshape(n_seq, -1), prm, alpha, n_seq)
        kp_l.append(kp.reshape(batch, seq, h_b, HEAD_DIM))
        vp_l.append(vp.reshape(batch, seq, h_b, HEAD_DIM))
        ks_l.append(kn.reshape(n_seq, 1, h_b, HEAD_DIM))
        vs_l.append(vn.reshape(n_seq, 1, h_b, HEAD_DIM))
        cs_l.append(cvs.reshape(n_seq, 1, h_a, HEAD_DIM))
    return (xp.reshape(batch, seq, d), xs.reshape(n_seq, 1, d), jnp.stack(kp_l), jnp.stack(vp_l),
            jnp.stack(ks_l), jnp.stack(vs_l), jnp.stack(cs_l))
```

```python
import functools
import math

import jax
import jax.numpy as jnp
from jax import lax
from jax.experimental import pallas as pl
from jax.experimental.pallas import tpu as pltpu

HEAD_DIM = 64
BLOCK = 256
PAGE = 128
CHUNK = 128
TOPK_BLOCKS = 3
N_KEYS = 128
P_HEADS = 8
KEY_DIM = 128
P_TOPK = 16
LN_EPS = 1e-5
ROPE_THETA = 10000.0
LANES = 128
NEG = -1e30

VMEM_LIMIT = 56 * 1024 * 1024

F32 = jnp.float32
BF16 = jnp.bfloat16


def _nt(a, b):
    return lax.dot_general(a, b, (((1,), (1,)), ((), ())), preferred_element_type=F32)


def _nn(a, b):
    return jnp.dot(a, b, preferred_element_type=F32)


def _split(a):
    hi = a.astype(BF16)
    lo = (a - hi.astype(F32)).astype(BF16)
    return hi, lo


def _gelu(x):
    return 0.5 * x * (1.0 + lax.erf(x * (1.0 / math.sqrt(2.0))))


def _cparams(sem):
    return pltpu.CompilerParams(dimension_semantics=sem, vmem_limit_bytes=VMEM_LIMIT)


def _rope_cols(a, cos, sin_signed):
    outs = []
    lane = lax.broadcasted_iota(jnp.int32, (1, LANES), 1)
    first_half = (lane % HEAD_DIM) < (HEAD_DIM // 2)
    for c in range(a.shape[1] // LANES):
        ac = a[:, c * LANES:(c + 1) * LANES]
        partner = jnp.where(first_half, pltpu.roll(ac, LANES - HEAD_DIM // 2, 1), pltpu.roll(ac, HEAD_DIM // 2, 1))
        outs.append(ac * cos + partner * sin_signed)
    return jnp.concatenate(outs, axis=1)


def _mix_core(x_ref, w_ref, cos_ref, sin_ref, avg_ref, g_ref, b_ref, bw, aw):
    z = _nn(x_ref[...].astype(BF16), w_ref[...])
    cos = cos_ref[...]
    sin = sin_ref[...]
    q = _rope_cols(z[:, 0:bw], cos, sin)
    k = _rope_cols(z[:, bw:2 * bw], cos, sin)
    v = z[:, 2 * bw:3 * bw]
    u = _gelu(z[:, 3 * bw:3 * bw + aw])
    gv = _gelu(z[:, 3 * bw + aw:3 * bw + 2 * aw])
    avg = avg_ref[...]
    gh, gl = _split(gv)
    mu = _nn(gh, avg) + _nn(gl, avg)
    cen = gv - mu
    sh, sl = _split(cen * cen)
    var = _nn(sh, avg) + _nn(sl, avg)
    cvn = cen * lax.rsqrt(var + LN_EPS) * g_ref[...] + b_ref[...]
    return q, k, v, u, cvn


def _mix_prompt_kernel(x_ref, w_ref, cos_ref, sin_ref, avg_ref, g_ref, b_ref, wsp_ref, bsp_ref,
                       q_ref, k_ref, v_ref, vt_ref, ch_ref, *, bw, aw):
    q, k, v, u, cvn = _mix_core(x_ref, w_ref, cos_ref, sin_ref, avg_ref, g_ref, b_ref, bw, aw)
    q_ref[...] = q
    k_ref[...] = k
    v_ref[...] = v
    vt_ref[0] = v.T.astype(BF16)
    lane = lax.broadcasted_iota(jnp.int32, (1, LANES), 1)
    low = lane < HEAD_DIM
    cvb = cvn.astype(BF16)
    for c in range(cvn.shape[0] // CHUNK):
        rows = slice(c * CHUNK, (c + 1) * CHUNK)
        for p in range(aw // LANES):
            cols = slice(p * LANES, (p + 1) * LANES)
            cp = cvb[rows, cols]
            mixed = jnp.where(low, _nn(wsp_ref[2 * p], cp), _nn(wsp_ref[2 * p + 1], cp)) + bsp_ref[:, cols]
            ch_ref[rows, cols] = (u[rows, cols] * mixed).astype(BF16)


def _mix_sample_kernel(x_ref, w_ref, cos_ref, sin_ref, avg_ref, g_ref, b_ref, w00_ref, b0_ref,
                       q_ref, k_ref, v_ref, cv_ref, ch_ref, *, bw, aw):
    q, k, v, u, cvn = _mix_core(x_ref, w_ref, cos_ref, sin_ref, avg_ref, g_ref, b_ref, bw, aw)
    q_ref[...] = q
    k_ref[...] = k
    v_ref[...] = v
    cv_ref[...] = cvn
    ch_ref[...] = (u * (cvn * w00_ref[...] + b0_ref[...])).astype(BF16)


def _rope_tables(pos):
    half = HEAD_DIM // 2
    inv_freq = ROPE_THETA ** (-jnp.arange(half, dtype=F32) / half)
    ang = pos.astype(F32)[:, None] * inv_freq[None, :]
    cos, sin = jnp.cos(ang), jnp.sin(ang)
    cos_t = jnp.concatenate([cos, cos, cos, cos], axis=1)
    sin_t = jnp.concatenate([-sin, sin, -sin, sin], axis=1)
    return cos_t, sin_t


def _head_avg_matrix(width):
    h = jnp.arange(width) // HEAD_DIM
    return jnp.where(h[:, None] == h[None, :], 1.0 / HEAD_DIM, 0.0).astype(BF16)


def _mix_prompt(x2d, w_in_b, cos_t, sin_t, avg, g_row, b_row, wsp_b, bsp_full, batch, seq, bw, aw, tt):
    n, d = x2d.shape
    tiles_per_seq = seq // tt
    grid = (n // tt,)
    full = lambda shape: pl.BlockSpec(shape, lambda i: (0,) * len(shape))
    tok = lambda w: pl.BlockSpec((tt, w), lambda i: (i, 0))
    return pl.pallas_call(
        functools.partial(_mix_prompt_kernel, bw=bw, aw=aw),
        grid=grid,
        in_specs=[tok(d), full(w_in_b.shape),
                  pl.BlockSpec((tt, LANES), lambda i: (i % tiles_per_seq, 0)),
                  pl.BlockSpec((tt, LANES), lambda i: (i % tiles_per_seq, 0)),
                  full(avg.shape), full(g_row.shape), full(b_row.shape), full(wsp_b.shape), full(bsp_full.shape)],
        out_specs=[tok(bw), tok(bw), tok(bw),
                   pl.BlockSpec((1, bw, tt), lambda i: (i // tiles_per_seq, 0, i % tiles_per_seq)),
                   tok(aw)],
        out_shape=[jax.ShapeDtypeStruct((n, bw), F32), jax.ShapeDtypeStruct((n, bw), F32),
                   jax.ShapeDtypeStruct((n, bw), F32), jax.ShapeDtypeStruct((batch, bw, seq), BF16),
                   jax.ShapeDtypeStruct((n, aw), BF16)],
        compiler_params=_cparams(("arbitrary",)),
        name="mix_prompt",
    )(x2d, w_in_b, cos_t, sin_t, avg, g_row, b_row, wsp_b, bsp_full)


def _mix_sample(x2d, w_in_b, cos_t, sin_t, avg, g_row, b_row, w00_row, b0_row, bw, aw):
    n, d = x2d.shape
    full = lambda shape: pl.BlockSpec(shape, lambda i: (0,) * len(shape))
    ins = (x2d, w_in_b, cos_t, sin_t, avg, g_row, b_row, w00_row, b0_row)
    return pl.pallas_call(
        functools.partial(_mix_sample_kernel, bw=bw, aw=aw),
        grid=(1,),
        in_specs=[full(a.shape) for a in ins],
        out_specs=[full((n, bw))] * 3 + [full((n, aw))] * 2,
        out_shape=[jax.ShapeDtypeStruct((n, bw), F32)] * 3
                  + [jax.ShapeDtypeStruct((n, aw), F32), jax.ShapeDtypeStruct((n, aw), BF16)],
        compiler_params=_cparams(("arbitrary",)),
        name="mix_sample",
    )(*ins)


KV_UNROLL = 4
MASK_BIAS = -30000.0


def _top_rows(sc, rows, k):
    sel = jnp.zeros(sc.shape, F32)
    big = float(sc.shape[0])
    for _ in range(k):
        m = jnp.max(sc, axis=0, keepdims=True)
        first = jnp.min(jnp.where(sc == m, rows, big), axis=0, keepdims=True)
        hit = rows == first
        sel = jnp.where(hit & (m > -jnp.inf), 1.0, sel)
        sc = jnp.where(hit, -jnp.inf, sc)
    return sel


def _moba_prompt_kernel(q_ref, k_ref, vt_ref, o_ref, kmean_scr, *, nblocks):
    i = pl.program_id(2)
    tag_rows = HEAD_DIM

    @pl.when(i == 0)
    def _():
        kmean_scr[...] = jnp.zeros(kmean_scr.shape, F32)
        for nb in range(nblocks):
            kmean_scr[nb:nb + 1, :] = jnp.sum(k_ref[0, nb * BLOCK:(nb + 1) * BLOCK, :], axis=0, keepdims=True) * (1.0 / BLOCK)

    lane = lax.broadcasted_iota(jnp.int32, (1, LANES), 1)
    q2 = q_ref[...]
    kmh, kml = _split(kmean_scr[...])
    blk = lax.broadcasted_iota(jnp.int32, (tag_rows, BLOCK), 0).astype(F32)
    zeros = jnp.zeros((tag_rows, BLOCK), F32)
    mine, tag_base, qa = [], [], []
    for h in range(2):
        mine.append((lane // HEAD_DIM) == h)
        tag_base.append(HEAD_DIM * (1 - h))
        qh, ql = _split(jnp.where(mine[h], q2, 0.0))
        sc = _nt(kmh, qh) + _nt(kmh, ql) + _nt(kml, qh)
        sc = jnp.where(blk < i.astype(F32), sc, -jnp.inf)
        bias_t = jnp.where(_top_rows(sc, blk, TOPK_BLOCKS) > 0.0, 0.0, MASK_BIAS)
        bias = jnp.concatenate([zeros, bias_t] if h == 0 else [bias_t, zeros], axis=0).T
        qa.append(jnp.where(mine[h], q2 * (1.0 / math.sqrt(HEAD_DIM)), bias).astype(BF16))

    key_row = lax.broadcasted_iota(jnp.int32, (BLOCK, BLOCK), 0)
    qry_col = lax.broadcasted_iota(jnp.int32, (BLOCK, BLOCK), 1)

    def step(n0, nblk, carry, own):
        start = pl.multiple_of(n0 * BLOCK, BLOCK)
        kbs = [k_ref[0, pl.ds(start + u * BLOCK, BLOCK), :].astype(BF16) for u in range(nblk)]
        out = []
        for h in range(2):
            m, l, acc = carry[h]
            ss = []
            for u in range(nblk):
                if own:
                    s = jnp.where(key_row <= qry_col, _nt(jnp.where(mine[h], kbs[u], jnp.zeros((), BF16)), qa[h]), NEG)
                else:
                    tag = (lane == (tag_base[h] + n0 + u)).astype(BF16)
                    s = _nt(jnp.where(mine[h], kbs[u], tag), qa[h])
                ss.append(s)
            mc = ss[0]
            for s in ss[1:]:
                mc = jnp.maximum(mc, s)
            m_new = jnp.maximum(m, jnp.max(mc, axis=0, keepdims=True))
            alpha = jnp.exp(m - m_new)
            ps = [jnp.exp(s - m_new) for s in ss]
            psum = ps[0]
            for p in ps[1:]:
                psum = psum + p
            pcat = jnp.concatenate([p.astype(BF16) for p in ps], axis=0) if nblk > 1 else ps[0].astype(BF16)
            vtb = vt_ref[0, h * HEAD_DIM:(h + 1) * HEAD_DIM, pl.ds(start, nblk * BLOCK)]
            out.append((m_new, alpha * l + jnp.sum(psum, axis=0, keepdims=True), alpha * acc + _nn(vtb, pcat)))
        return tuple(out)

    init = tuple((jnp.full((1, BLOCK), NEG, F32), jnp.zeros((1, BLOCK), F32), jnp.zeros((HEAD_DIM, BLOCK), F32))
                 for _ in range(2))
    n_full = lax.div(i, KV_UNROLL)
    carry = lax.fori_loop(0, n_full, lambda g, c: step(g * KV_UNROLL, KV_UNROLL, c, False), init)
    done = n_full * KV_UNROLL
    n_pair = lax.div(i - done, 2)
    carry = lax.fori_loop(0, n_pair, lambda g, c: step(done + g * 2, 2, c, False), carry)
    carry = lax.fori_loop(done + n_pair * 2, i, lambda j, c: step(j, 1, c, False), carry)
    carry = step(i, 1, carry, True)
    o_ref[...] = jnp.concatenate([acc / l for _, l, acc in carry], axis=0).T.astype(BF16)


def _moba_prompt(q, k, vt, batch, seq, bw):
    n = q.shape[0]
    nblocks = seq // BLOCK
    assert nblocks <= HEAD_DIM
    pairs = bw // LANES
    k3 = k.reshape(batch, seq, bw)
    return pl.pallas_call(
        functools.partial(_moba_prompt_kernel, nblocks=nblocks),
        grid=(batch, pairs, nblocks),
        in_specs=[pl.BlockSpec((BLOCK, LANES), lambda b, p, i: (b * nblocks + i, p)),
                  pl.BlockSpec((1, seq, LANES), lambda b, p, i: (b, 0, p)),
                  pl.BlockSpec((1, LANES, seq), lambda b, p, i: (b, p, 0))],
        out_specs=pl.BlockSpec((BLOCK, LANES), lambda b, p, i: (b * nblocks + i, p)),
        out_shape=jax.ShapeDtypeStruct((n, bw), BF16),
        scratch_shapes=[pltpu.VMEM((HEAD_DIM, LANES), F32)],
        compiler_params=_cparams(("arbitrary", "arbitrary", "arbitrary")),
        name="moba_prompt",
    )(q, k3, vt)


PAGES_PER_STEP = 16
PAGES_PER_BLOCK = BLOCK // PAGE


def _pages_kernel(pt_ref, q_ref, *rest, nblocks):
    page_refs = rest[:PAGES_PER_STEP]
    sel_ref = rest[PAGES_PER_STEP]
    part_scr = rest[PAGES_PER_STEP + 1]
    g = pl.program_id(1)
    bps = PAGES_PER_STEP // PAGES_PER_BLOCK
    qcol = q_ref[0]
    for b in range(bps):
        acc = page_refs[b * PAGES_PER_BLOCK][0, 0]
        for r in range(1, PAGES_PER_BLOCK):
            acc = acc + page_refs[b * PAGES_PER_BLOCK + r][0, 0]
        part_scr[g * bps + b] = jnp.sum(acc * qcol, axis=1)

    @pl.when(g == pl.num_programs(1) - 1)
    def _():
        sc = jnp.sum(part_scr[...], axis=-1, keepdims=True) * (1.0 / BLOCK)
        rows = lax.broadcasted_iota(jnp.int32, sc.shape, 0).astype(F32)
        picks = []
        for _ in range(TOPK_BLOCKS):
            m = jnp.max(sc, axis=0, keepdims=True)
            first = jnp.min(jnp.where(sc == m, rows, float(nblocks)), axis=0, keepdims=True)
            picks.append(first)
            sc = jnp.where(rows == first, -jnp.inf, sc)
        sel_ref[0] = jnp.concatenate(picks, axis=0).astype(jnp.int32)


def _sample_block_choice(cache_kt, layer, pt_flat, q_s, n_seq, n_pages, heads):
    nblocks = n_pages // PAGES_PER_BLOCK
    assert nblocks >= TOPK_BLOCKS
    steps = n_pages // PAGES_PER_STEP

    def page_spec(r):
        return pl.BlockSpec((1, 1, heads, HEAD_DIM, PAGE),
                            lambda s, g, pt: (layer, pt[s * n_pages + g * PAGES_PER_STEP + r], 0, 0, 0))

    grid_spec = pltpu.PrefetchScalarGridSpec(
        num_scalar_prefetch=1,
        grid=(n_seq, steps),
        in_specs=[pl.BlockSpec((1, heads, HEAD_DIM, 1), lambda s, g, pt: (s, 0, 0, 0))]
                 + [page_spec(r) for r in range(PAGES_PER_STEP)],
        out_specs=pl.BlockSpec((1, TOPK_BLOCKS, heads, 1), lambda s, g, pt: (s, 0, 0, 0)),
        scratch_shapes=[pltpu.VMEM((nblocks, heads, PAGE), F32)],
    )
    return pl.pallas_call(
        functools.partial(_pages_kernel, nblocks=nblocks),
        grid_spec=grid_spec,
        out_shape=jax.ShapeDtypeStruct((n_seq, TOPK_BLOCKS, heads, 1), jnp.int32),
        compiler_params=_cparams(("arbitrary", "arbitrary")),
        name="sample_block_choice",
    )(pt_flat, q_s.reshape(n_seq, heads, HEAD_DIM, 1), *([cache_kt] * PAGES_PER_STEP))


N_FETCH = TOPK_BLOCKS * PAGES_PER_BLOCK


def _sattn_kernel(pt_ref, sel_ref, q_ref, kn_ref, vn_ref, ck_hbm, cv_hbm, o_ref, kbuf, vbuf, sem,
                  *, layer, n_pages, heads):
    s = pl.program_id(0)
    n = pl.num_programs(0)

    def copies(seq, slot):
        out = []
        for h in range(heads):
            for r in range(N_FETCH):
                pick, pg = divmod(r, PAGES_PER_BLOCK)
                blk = sel_ref[(seq * TOPK_BLOCKS + pick) * heads + h]
                page = pt_ref[seq * n_pages + blk * PAGES_PER_BLOCK + pg]
                dst = pl.ds(r * PAGE, PAGE)
                out.append(pltpu.make_async_copy(ck_hbm.at[layer, page, h], kbuf.at[slot, h, :, dst], sem.at[0, slot]))
                out.append(pltpu.make_async_copy(cv_hbm.at[layer, page, h], vbuf.at[slot, h, :, dst], sem.at[1, slot]))
        return out

    slot = lax.rem(s, 2)

    @pl.when(s == 0)
    def _():
        for c in copies(0, 0):
            c.start()

    @pl.when(s + 1 < n)
    def _():
        for c in copies(s + 1, 1 - slot):
            c.start()

    for c in copies(s, slot):
        c.wait()

    q3 = q_ref[0]
    kn3 = kn_ref[0]
    vn3 = vn_ref[0]
    outs = []
    for h in range(heads):
        qh = q3[h:h + 1, :] * (1.0 / math.sqrt(HEAD_DIM))
        qb = qh.astype(BF16)
        sc = _nn(jnp.broadcast_to(qb, (8, HEAD_DIM)), kbuf[slot, h].astype(BF16))[0:1, :]
        s_own = jnp.sum(qb.astype(F32) * kn3[h:h + 1, :].astype(BF16).astype(F32), axis=1, keepdims=True)
        m = jnp.maximum(jnp.max(sc, axis=1, keepdims=True), s_own)
        p = jnp.exp(sc - m)
        p_own = jnp.exp(s_own - m)
        l = jnp.sum(p, axis=1, keepdims=True) + p_own
        pv = _nt(jnp.broadcast_to(p, (8, N_FETCH * PAGE)).astype(BF16), vbuf[slot, h].astype(BF16))[0:1, :]
        outs.append((pv + p_own * vn3[h:h + 1, :].astype(BF16).astype(F32)) / l)
    o_ref[0] = jnp.concatenate(outs, axis=0).astype(BF16)


def _sample_attention(cache_k, cache_v, layer, pt_flat, sel_flat, q_s, k_s, v_s, n_seq, n_pages, heads):
    row = pl.BlockSpec((1, heads, HEAD_DIM), lambda s, pt, sel: (s, 0, 0))
    anyspec = pl.BlockSpec(memory_space=pl.ANY)
    grid_spec = pltpu.PrefetchScalarGridSpec(
        num_scalar_prefetch=2,
        grid=(n_seq,),
        in_specs=[row, row, row, anyspec, anyspec],
        out_specs=row,
        scratch_shapes=[pltpu.VMEM((2, heads, HEAD_DIM, N_FETCH * PAGE), F32),
                        pltpu.VMEM((2, heads, HEAD_DIM, N_FETCH * PAGE), F32),
                        pltpu.SemaphoreType.DMA((2, 2))],
    )
    r3 = lambda a: a.reshape(n_seq, heads, HEAD_DIM)
    return pl.pallas_call(
        functools.partial(_sattn_kernel, layer=layer, n_pages=n_pages, heads=heads),
        grid_spec=grid_spec,
        out_shape=jax.ShapeDtypeStruct((n_seq, heads, HEAD_DIM), BF16),
        compiler_params=_cparams(("arbitrary",)),
        name="sample_attention",
    )(pt_flat, sel_flat, r3(q_s), r3(k_s), r3(v_s), cache_k, cache_v).reshape(n_seq, heads * HEAD_DIM)


def _cand_layout():
    a_of, b_of, valid = [], [], []
    def add(a, b):
        a_of.append(a)
        b_of.append(b)
        valid.append((a + 1) * (b + 1) <= P_TOPK)
    for b in range(16):
        add(0, b)
    for a in range(1, 8):
        for b in range(8):
            add(a, b)
    for a in range(8, 16):
        add(a, 0)
    return a_of, b_of, valid


_CAND_A, _CAND_B, _CAND_VALID = _cand_layout()
N_CAND = len(_CAND_A)
_BIG_IDX = float(1 << 20)


def _take_max(s, idx, exact):
    m = jnp.max(s, axis=0, keepdims=True)
    if not exact:
        return m, s == m
    first = jnp.min(jnp.where(s == m, idx, _BIG_IDX), axis=0, keepdims=True)
    return m, idx == first


def _top16_ranks(s, idx, exact):
    rank = jnp.full(s.shape, float(P_TOPK), F32)
    ts = []
    for r in range(P_TOPK):
        m, hit = _take_max(s, idx, exact)
        rank = jnp.where(hit, float(r), rank)
        s = jnp.where(hit, -jnp.inf, s)
        ts.append(m)
    return rank, jnp.concatenate(ts, axis=0)


_RANK_SUM = float((N_KEYS - P_TOPK) * P_TOPK + P_TOPK * (P_TOPK - 1) // 2)


def _post_kernel(attn_ref, ch_ref, x_ref, woa_ref, wob_ref, g1_ref, b1_ref, wq_ref, sk_ref,
                 ca_ref, cidx_ref, cval_ref,
                 x1t_ref, r1_ref, p1_ref, cut_ref, p0_ref, s_scr, *, alpha):
    z = _nn(attn_ref[...], woa_ref[...]) + _nn(ch_ref[...], wob_ref[...]) + alpha * x_ref[...]
    mu = jnp.mean(z, axis=1, keepdims=True)
    cen = z - mu
    var = jnp.mean(cen * cen, axis=1, keepdims=True)
    x1 = cen * lax.rsqrt(var + LN_EPS) * g1_ref[...] + b1_ref[...]
    x1t_ref[...] = x1.T
    xb = x1.astype(BF16)
    t = x1.shape[0]

    def route(h, exact):
        if True:
            def per_tile(lt, bad):
                cols = pl.ds(pl.multiple_of(lt * LANES, LANES), LANES)
                key_row = lax.broadcasted_iota(jnp.int32, (N_KEYS, LANES), 0).astype(F32)
                s0 = s_scr[0, :, cols]
                s1 = s_scr[1, :, cols]
                rank0, ts0 = _top16_ranks(s0, key_row, exact)
                rank1, ts1 = _top16_ranks(s1, key_row, exact)
                pieces = [ts0[0:1, :] + ts1]
                for a in range(1, 8):
                    pieces.append(ts0[a:a + 1, :] + ts1[0:8, :])
                pieces.append(ts0[8:16, :] + ts1[0:1, :])
                cand = jnp.where(cval_ref[...] > 0.0, jnp.concatenate(pieces, axis=0), -jnp.inf)
                cidx = cidx_ref[...]
                ca = ca_ref[...]
                a_iota = lax.broadcasted_iota(jnp.int32, (P_TOPK, LANES), 0).astype(F32)
                cut = jnp.zeros((P_TOPK, LANES), F32)
                mass = jnp.zeros((1, LANES), F32)
                for r in range(P_TOPK):
                    m, hit = _take_max(cand, cidx, exact)
                    cand = jnp.where(hit, -jnp.inf, cand)
                    a_pick = jnp.max(jnp.where(hit, ca, -1.0), axis=0, keepdims=True)
                    cut = cut + jnp.where(a_iota == a_pick, 1.0, 0.0)
                    m0 = m if r == 0 else m0
                    mass = mass + jnp.exp(m - m0)
                cutd = jnp.zeros((N_KEYS, LANES), F32)
                for a in range(P_TOPK):
                    cutd = jnp.where(rank0 == float(a), cut[a:a + 1, :], cutd)
                p0 = jnp.where(rank0 < float(P_TOPK), jnp.exp(s0 - ts0[0:1, :]), 0.0)
                p1 = jnp.where(rank1 < float(P_TOPK), jnp.exp(s1 - ts1[0:1, :]), 0.0) / mass
                r1_ref[h, :, cols] = rank1.astype(BF16)
                p1_ref[h, :, cols] = p1.astype(BF16)
                cut_ref[h, :, cols] = cutd
                p0_ref[h, :, cols] = p0
                if exact:
                    return bad
                for rank in (rank0, rank1):
                    bad = jnp.maximum(bad, jnp.abs(jnp.sum(rank, axis=0, keepdims=True) - _RANK_SUM))
                left = jnp.sum(jnp.where(cand == -jnp.inf, 0.0, 1.0), axis=0, keepdims=True)
                return jnp.maximum(bad, jnp.abs(left - float(sum(_CAND_VALID) - P_TOPK)))

            return lax.fori_loop(0, t // LANES, per_tile, jnp.zeros((1, LANES), F32))

    def per_head(h, _):
        for c in range(2):
            rows = pl.ds(pl.multiple_of((h * 2 + c) * KEY_DIM, KEY_DIM), KEY_DIM)
            qt = _nt(wq_ref[rows, :], xb)
            s_scr[c] = _nn(sk_ref[h * 2 + c], qt.astype(BF16))
        tie_alarm = route(h, False)

        @pl.when(jnp.max(tie_alarm) > 0.0)
        def _():
            route(h, True)

        return 0

    lax.fori_loop(0, P_HEADS, per_head, 0)


def _post(attn, ch, x2d, woa, wob, g1, b1, wq_b, sk_b, alpha, tb):
    n, d = x2d.shape
    bw, aw = attn.shape[1], ch.shape[1]
    assert tb % LANES == 0
    wide = lambda v: jnp.broadcast_to(jnp.asarray(v, F32).reshape(N_CAND, 1), (N_CAND, LANES))
    ca = wide(_CAND_A)
    cidx = wide([a * P_TOPK + b for a, b in zip(_CAND_A, _CAND_B)])
    cval = wide(_CAND_VALID)
    full = lambda a: pl.BlockSpec(a.shape, lambda i: (0,) * a.ndim)
    tok = lambda w: pl.BlockSpec((tb, w), lambda i: (i, 0))
    gate_spec = pl.BlockSpec((P_HEADS, N_KEYS, tb), lambda i: (0, 0, i))
    return pl.pallas_call(
        functools.partial(_post_kernel, alpha=alpha),
        grid=(n // tb,),
        in_specs=[tok(bw), tok(aw), tok(d), full(woa), full(wob), full(g1), full(b1), full(wq_b), full(sk_b),
                  full(ca), full(cidx), full(cval)],
        out_specs=[pl.BlockSpec((d, tb), lambda i: (0, i)), gate_spec, gate_spec, gate_spec, gate_spec],
        out_shape=[jax.ShapeDtypeStruct((d, n), F32),
                   jax.ShapeDtypeStruct((P_HEADS, N_KEYS, n), BF16), jax.ShapeDtypeStruct((P_HEADS, N_KEYS, n), BF16),
                   jax.ShapeDtypeStruct((P_HEADS, N_KEYS, n), F32), jax.ShapeDtypeStruct((P_HEADS, N_KEYS, n), F32)],
        scratch_shapes=[pltpu.VMEM((2, N_KEYS, tb), F32)],
        compiler_params=_cparams(("arbitrary",)),
        name="post",
    )(attn, ch, x2d, woa, wob, g1, b1, wq_b, sk_b, ca, cidx, cval)


EXPERT_TILE = 1024


def _peer_kernel(x1t_ref, r1_ref, p1_ref, cut_ref, p0_ref, u_ref, vt_ref, vtp_ref, p_ref, g2_ref, b2_ref,
                 gwt_ref, gb_ref, pwt_ref, y_ref, acc_scr, xb_scr, ga_scr, gb_scr, *, alpha):
    e = pl.program_id(1)
    last = pl.num_programs(1) - 1
    half = EXPERT_TILE // 2

    @pl.when(e == 0)
    def _():
        acc_scr[...] = jnp.zeros(acc_scr.shape, F32)
        xb_scr[...] = x1t_ref[...].astype(BF16)
        gb_scr[...] = jnp.zeros(gb_scr.shape, BF16)

    def gate(ht, part, out_scr):
        for k in range(half // N_KEYS):
            ib = part * (half // N_KEYS) + k
            w = None
            for h in range(P_HEADS):
                cut_row = cut_ref[h, ib:ib + 1, :].astype(BF16)
                p0_row = p0_ref[h, ib:ib + 1, :].astype(BF16)
                term = jnp.where(r1_ref[h] < cut_row, p1_ref[h], jnp.zeros((), BF16)) * p0_row
                w = term if w is None else w + term
            rows = slice(k * N_KEYS, (k + 1) * N_KEYS)
            out_scr[rows, :] = (_gelu(ht[rows, :]) * w.astype(F32)).astype(BF16)

    @pl.when(e < last)
    def _():
        ht_a = _nn(u_ref[0:half, :], xb_scr[...])
        gate(ht_a, 0, ga_scr)
        ht_b = _nn(u_ref[half:EXPERT_TILE, :], xb_scr[...])
        acc_scr[...] += _nn(vt_ref[...], ga_scr[...]) + _nn(vtp_ref[...], gb_scr[...])
        gate(ht_b, 1, gb_scr)

    @pl.when(e == last)
    def _():
        z = alpha * x1t_ref[...] + acc_scr[...] + _nn(vtp_ref[...], gb_scr[...])
        mu = jnp.mean(z, axis=0, keepdims=True)
        cen = z - mu
        var = jnp.mean(cen * cen, axis=0, keepdims=True)
        x2 = cen * lax.rsqrt(var + LN_EPS) * g2_ref[...] + b2_ref[...]
        gate_v = jax.nn.sigmoid(_nn(gwt_ref[...], x2.astype(BF16)) + gb_ref[...])
        ple = _nt(pwt_ref[...], p_ref[...].astype(BF16))
        y_ref[...] = (x2 + gate_v * ple).T


def _peer(x1t, r1, p1, cut, p0, u_b, vt_b, p2d, g2c, b2c, gwt, gbc, pwt, alpha, tb):
    d, n = x1t.shape
    n_exp = u_b.shape[0]
    kpt = EXPERT_TILE // N_KEYS
    half = EXPERT_TILE // 2
    n_tiles = n_exp // EXPERT_TILE
    tile = lambda e: jnp.minimum(e, n_tiles - 1)
    tcol = lambda rows: pl.BlockSpec((rows, tb), lambda t, e: (0, t))
    full = lambda a: pl.BlockSpec(a.shape, lambda t, e: (0,) * a.ndim)
    return pl.pallas_call(
        functools.partial(_peer_kernel, alpha=alpha),
        grid=(n // tb, n_tiles + 1),
        in_specs=[tcol(d),
                  pl.BlockSpec((P_HEADS, N_KEYS, tb), lambda t, e: (0, 0, t)),
                  pl.BlockSpec((P_HEADS, N_KEYS, tb), lambda t, e: (0, 0, t)),
                  pl.BlockSpec((P_HEADS, kpt, tb), lambda t, e: (0, tile(e), t)),
                  pl.BlockSpec((P_HEADS, kpt, tb), lambda t, e: (0, tile(e), t)),
                  pl.BlockSpec((EXPERT_TILE, d), lambda t, e: (tile(e), 0)),
                  pl.BlockSpec((d, half), lambda t, e: (0, 2 * tile(e))),
                  pl.BlockSpec((d, half), lambda t, e: (0, jnp.maximum(2 * e - 1, 0))),
                  pl.BlockSpec((tb, p2d.shape[1]), lambda t, e: (t, 0)),
                  full(g2c), full(b2c), full(gwt), full(gbc), full(pwt)],
        out_specs=pl.BlockSpec((tb, d), lambda t, e: (t, 0)),
        out_shape=jax.ShapeDtypeStruct((n, d), F32),
        scratch_shapes=[pltpu.VMEM((d, tb), F32), pltpu.VMEM((d, tb), BF16),
                        pltpu.VMEM((half, tb), BF16), pltpu.VMEM((half, tb), BF16)],
        compiler_params=_cparams(("arbitrary", "arbitrary")),
        name="peer",
    )(x1t, r1, p1, cut, p0, u_b, vt_b, vt_b, p2d, g2c, b2c, gwt, gbc, pwt)


def _finish(x2d, attn, ch, p2d, prm, alpha, tb):
    x1t, r1, p1, cut, p0 = _post(attn, ch, x2d, prm["woa"], prm["wob"], prm["g1"], prm["b1"], prm["wq_b"],
                                 prm["sk_b"], alpha, tb)
    return _peer(x1t, r1, p1, cut, p0, prm["u_b"], prm["vt_b"], p2d, prm["g2c"], prm["b2c"], prm["gwt"],
                 prm["gbc"], prm["pwt"], alpha, tb)


def kernel(x_prompt, x_sample, cache_k, cache_v, page_table, p_prompt, p_sample, w_in, vnorm_g, vnorm_b, w_spatial, b_spatial, w_out, ln1_g, ln1_b, peer_wq, peer_subkeys, peer_u, peer_v, ln2_g, ln2_b, ple_w, ple_gate_w, ple_gate_b):
    batch, seq, d = x_prompt.shape
    n_seq, t_new, _ = x_sample.shape
    depth = w_in.shape[0]
    n_pages = page_table.shape[1]
    past = n_pages * PAGE
    h_a = vnorm_g.shape[1]
    aw = h_a * HEAD_DIM
    bw = (w_in.shape[2] - 2 * aw) // 3
    h_b = bw // HEAD_DIM
    assert t_new == 1 and seq % BLOCK == 0 and past % BLOCK == 0 and n_pages % PAGES_PER_STEP == 0
    assert bw % LANES == 0 and aw % LANES == 0
    assert cache_k.shape[2:] == (PAGE, h_b, HEAD_DIM)
    alpha = (2.0 * depth) ** 0.25

    tt = 512 if seq % 512 == 0 else BLOCK
    tb_p = 512 if (batch * seq) % 512 == 0 else BLOCK

    cos_p, sin_p = _rope_tables(jnp.arange(seq))
    cos_s, sin_s = _rope_tables(jnp.full((n_seq,), past, jnp.int32))
    avg = _head_avg_matrix(aw)
    tril = jnp.tril(jnp.ones((CHUNK, CHUNK), F32))
    pt_flat = page_table.reshape(-1).astype(jnp.int32)
    cache_kt = jnp.transpose(cache_k, (0, 1, 3, 4, 2))
    cache_vt = jnp.transpose(cache_v, (0, 1, 3, 4, 2))

    xp = x_prompt.reshape(batch * seq, d)
    xs = x_sample.reshape(n_seq, d)
    kp_l, vp_l, ks_l, vs_l, cs_l = [], [], [], [], []
    for i in range(depth):
        w_in_b = w_in[i].astype(BF16)
        g_row = vnorm_g[i].reshape(1, aw)
        b_row = vnorm_b[i].reshape(1, aw)
        wsp_b = (w_spatial[i] * tril).astype(BF16)
        bsp_full = jnp.repeat(b_spatial[i].T, HEAD_DIM, axis=1)
        w00_row = jnp.repeat(w_spatial[i][:, 0, 0], HEAD_DIM).reshape(1, aw)
        b0_row = jnp.repeat(b_spatial[i][:, 0], HEAD_DIM).reshape(1, aw)
        prm = dict(
            woa=w_out[i][:bw].astype(BF16), wob=w_out[i][bw:].astype(BF16),
            g1=ln1_g[i].reshape(1, d), b1=ln1_b[i].reshape(1, d),
            wq_b=peer_wq[i].T.astype(BF16),
            sk_b=peer_subkeys[i].reshape(P_HEADS * 2, N_KEYS, KEY_DIM).astype(BF16),
            u_b=peer_u[i].astype(BF16), vt_b=peer_v[i].T.astype(BF16),
            g2c=ln2_g[i].reshape(d, 1), b2c=ln2_b[i].reshape(d, 1),
            gwt=ple_gate_w[i].T.astype(BF16), gbc=ple_gate_b[i].reshape(d, 1), pwt=ple_w[i].T.astype(BF16))

        qp, kp, vp, vtp, chp = _mix_prompt(xp, w_in_b, cos_p, sin_p, avg, g_row, b_row, wsp_b, bsp_full,
                                           batch, seq, bw, aw, tt)
        attn_p = _moba_prompt(qp, kp, vtp, batch, seq, bw)
        qs, kn, vn, cvs, chs = _mix_sample(xs, w_in_b, cos_s, sin_s, avg, g_row, b_row, w00_row, b0_row, bw, aw)
        sel = _sample_block_choice(cache_kt, i, pt_flat, qs, n_seq, n_pages, h_b)
        attn_s = _sample_attention(cache_kt, cache_vt, i, pt_flat, sel.reshape(-1), qs, kn, vn, n_seq, n_pages, h_b)

        xp = _finish(xp, attn_p, chp, p_prompt[i].reshape(batch * seq, -1), prm, alpha, tb_p)
        xs = _finish(xs, attn_s, chs, p_sample[i].reshape(n_seq, -1), prm, alpha, n_seq)
        kp_l.append(kp.reshape(batch, seq, h_b, HEAD_DIM))
        vp_l.append(vp.reshape(batch, seq, h_b, HEAD_DIM))
        ks_l.append(kn.reshape(n_seq, 1, h_b, HEAD_DIM))
        vs_l.append(vn.reshape(n_seq, 1, h_b, HEAD_DIM))
        cs_l.append(cvs.reshape(n_seq, 1, h_a, HEAD_DIM))
    return (xp.reshape(batch, seq, d), xs.reshape(n_seq, 1, d), jnp.stack(kp_l), jnp.stack(vp_l),
            jnp.stack(ks_l), jnp.stack(vs_l), jnp.stack(cs_l))
```
